```python
import math
import jax
import jax.numpy as jnp
from jax import lax
import numpy as np

D_MODEL = 4096
BATCH = 4
SEQ = 2048
DEPTH = 4
DEC_BATCH = 8
DEC_SEQ = 4
PAST_LEN = 8192
PAGE_SIZE = 128

N_HEADS = 16
HEAD_DIM = 128
N_KV_HEADS = 4
Q_PER_KV = N_HEADS // N_KV_HEADS
ATT_DIM = N_HEADS * HEAD_DIM
KV_DIM = N_KV_HEADS * HEAD_DIM
ROT_DIM = HEAD_DIM // 4
IDX_HEADS = 8
IDX_DIM = 128
IDX_ROT = IDX_DIM // 4
TOPK_MAX = 256
Q_BLOCK = 128
ROPE_THETA = 500000.0
ATT_SCALE = HEAD_DIM ** -0.5
IDX_SCALE = (IDX_DIM * IDX_HEADS) ** -0.5
SSM_DIM = 2048
SSM_HEAD_DIM = 64
SSM_HEADS = SSM_DIM // SSM_HEAD_DIM
SSM_GROUPS = 4
SSM_STATE = 128
SSM_CHUNK = 128
CONV_W = 4
XBC_DIM = SSM_DIM + 2 * SSM_GROUPS * SSM_STATE
GM_DIM = 2048
GM_GROUPS = 16
GM_CHUNK = 128
N_BRANCH = 3
BRANCH_DIM = 2048
N_EXPERTS = 32
TOP_K = 4
EXPERT_DIM = 1024
SWIGLU_ALPHA = 1.702
SWIGLU_LIMIT = 7.0
MOE_BLOCK = 128
DN_ALPHA = (2 * DEPTH) ** 0.25
DN_BETA = (8 * DEPTH) ** -0.25
LN_EPS = 1e-5
RMS_EPS = 1e-5
IN_SPLITS = (ATT_DIM, KV_DIM, KV_DIM, IDX_HEADS * IDX_DIM, IDX_DIM, IDX_HEADS, SSM_DIM, XBC_DIM, SSM_HEADS, 2 * GM_DIM)
IN_COLS = sum(IN_SPLITS)

kernel_name = 'hybrid_dsa_ssd_gmlp_moe_step'


def _offsets(sizes):
    out, acc = [], 0
    for s in sizes[:-1]:
        acc += s
        out.append(acc)
    return out


def layer_norm(x, g, b):
    xf = x.astype(jnp.float32)
    mu = jnp.mean(xf, axis=-1, keepdims=True)
    var = jnp.mean(jnp.square(xf - mu), axis=-1, keepdims=True)
    return ((xf - mu) * lax.rsqrt(var + LN_EPS) * g + b).astype(x.dtype)


def rope(x, pos, rot):
    half = rot // 2
    inv = ROPE_THETA ** (-jnp.arange(half, dtype=jnp.float32) / half)
    ang = pos.astype(jnp.float32)[:, None] * inv
    cos = jnp.cos(ang)[:, None, :]
    sin = jnp.sin(ang)[:, None, :]
    xf = x.astype(jnp.float32)
    x1, x2, rest = xf[..., :half], xf[..., half:rot], xf[..., rot:]
    return jnp.concatenate([x1 * cos - x2 * sin, x2 * cos + x1 * sin, rest], axis=-1).astype(x.dtype)


def dsa_block(q, qi, wi, qpos, k, v, ki):
    b, t = q.shape[:2]
    n_keys = k.shape[1]
    topk = min(TOPK_MAX, n_keys // 4)
    kpos = jnp.arange(n_keys, dtype=jnp.int32)
    visible = kpos[None, :] <= qpos[:, None]
    dots = jnp.einsum('bthd,bsd->bths', qi, ki, preferred_element_type=jnp.float32)
    score = jnp.einsum('bths,bth->bts', jax.nn.relu(dots), wi.astype(jnp.float32)) * IDX_SCALE
    score = jnp.where(visible[None], score, -jnp.inf)
    _, sel = lax.top_k(score, topk)
    keep = sel <= qpos[None, :, None]
    bidx = jnp.arange(b)[:, None, None]
    kg = k[bidx, sel]
    vg = v[bidx, sel]
    logits = jnp.einsum('btkgd,btskd->btkgs', q, kg, preferred_element_type=jnp.float32) * ATT_SCALE
    logits = jnp.where(keep[:, :, None, None, :], logits, -jnp.inf)
    p = jax.nn.softmax(logits, axis=-1)
    o = jnp.einsum('btkgs,btskd->btkgd', p.astype(vg.dtype), vg)
    return o.reshape(b, t, ATT_DIM)


def dsa_attention(q, qi, wi, qpos, k, v, ki):
    b, t = q.shape[:2]
    if t <= Q_BLOCK or t % Q_BLOCK:
        return dsa_block(q, qi, wi, qpos, k, v, ki)
    nb = t // Q_BLOCK

    def blocks(a):
        return jnp.moveaxis(a.reshape((b, nb, Q_BLOCK) + a.shape[2:]), 1, 0)

    out = lax.map(lambda a: dsa_block(a[0], a[1], a[2], a[3], k, v, ki),
                  (blocks(q), blocks(qi), blocks(wi), qpos.reshape(nb, Q_BLOCK)))
    return jnp.moveaxis(out, 0, 1).reshape(b, t, ATT_DIM)


def ssd_scan(x, dt, a, bm, cm, h0, chunk):
    f32 = jnp.float32
    b, t = x.shape[:2]
    nc = t // chunk
    G, E, P, N = SSM_GROUPS, SSM_HEADS // SSM_GROUPS, SSM_HEAD_DIM, SSM_STATE
    xdt = (x.astype(f32) * dt[..., None]).reshape(b, nc, chunk, G, E, P)
    cum = jnp.cumsum((dt * a).reshape(b, nc, chunk, G, E), axis=2)
    bc = bm.astype(f32).reshape(b, nc, chunk, G, N)
    cc = cm.astype(f32).reshape(b, nc, chunk, G, N)
    causal = jnp.tril(jnp.ones((chunk, chunk), dtype=bool))[:, :, None, None]
    decay = jnp.exp(jnp.where(causal, cum[:, :, :, None] - cum[:, :, None, :], -jnp.inf))
    cb = jnp.einsum('bclgn,bcsgn->bclsg', cc, bc)
    y_diag = jnp.einsum('bclsge,bcsgep->bclgep', cb[..., None] * decay, xdt)
    to_end = jnp.exp(cum[:, :, -1:] - cum)
    states = jnp.einsum('bcsgen,bcsgep->bcgepn', bc[:, :, :, :, None, :] * to_end[..., None], xdt)
    chunk_decay = jnp.exp(cum[:, :, -1])

    def step(h, inp):
        st, dc = inp
        return h * dc[..., None, None] + st, h

    h_last, h_prev = lax.scan(step, h0.astype(f32).reshape(b, G, E, P, N),
                              (jnp.moveaxis(states, 1, 0), jnp.moveaxis(chunk_decay, 1, 0)))
    h_prev = jnp.moveaxis(h_prev, 0, 1)
    y_off = jnp.einsum('bclgen,bcgepn->bclgep', cc[..., None, :] * jnp.exp(cum)[..., None], h_prev)
    y = (y_diag + y_off).reshape(b, t, SSM_HEADS, P)
    return y, h_last.reshape(b, SSM_HEADS, P, N)


def ssd_mixer(z, xbc, dt_raw, conv_prev, conv_w, conv_b, dt_bias, a_log, d_skip, norm_g, h0):
    b, t, _ = xbc.shape
    xp = jnp.concatenate([conv_prev.astype(xbc.dtype), xbc], axis=1)
    conv = conv_b + sum(xp[:, i:i + t] * conv_w[i] for i in range(CONV_W))
    conv_new = xp[:, t:]
    u = jax.nn.silu(conv)
    xs, bm, cm = jnp.split(u, [SSM_DIM, SSM_DIM + SSM_GROUPS * SSM_STATE], axis=-1)
    xs = xs.reshape(b, t, SSM_HEADS, SSM_HEAD_DIM)
    bm = bm.reshape(b, t, SSM_GROUPS, SSM_STATE)
    cm = cm.reshape(b, t, SSM_GROUPS, SSM_STATE)
    dt = jax.nn.softplus(dt_raw.astype(jnp.float32) + dt_bias.astype(jnp.float32))
    a = -jnp.exp(a_log.astype(jnp.float32))
    chunk = SSM_CHUNK if t % SSM_CHUNK == 0 else t
    y, h_new = ssd_scan(xs, dt, a, bm, cm, h0, chunk)
    y = y + d_skip.astype(jnp.float32)[:, None] * xs.astype(jnp.float32)
    y = y.reshape(b, t, SSM_DIM) * jax.nn.silu(z.astype(jnp.float32))
    yg = y.reshape(b, t, SSM_GROUPS, SSM_DIM // SSM_GROUPS)
    yg = yg * lax.rsqrt(jnp.mean(jnp.square(yg), axis=-1, keepdims=True) + RMS_EPS)
    y = yg.reshape(b, t, SSM_DIM) * norm_g
    return y.astype(z.dtype), h_new.astype(h0.dtype), conv_new


def gmlp_mixer(gm, ln_g, ln_b, ws, bs):
    u, v = jnp.split(gm, 2, axis=-1)
    v = layer_norm(v, ln_g, ln_b)
    b, t, _ = v.shape
    nc = -(-t // GM_CHUNK)
    vp = jnp.pad(v, ((0, 0), (0, nc * GM_CHUNK - t), (0, 0)))
    vp = vp.reshape(b, nc, GM_CHUNK, GM_GROUPS, GM_DIM // GM_GROUPS)
    causal = jnp.tril(jnp.ones((GM_CHUNK, GM_CHUNK), dtype=bool))
    wm = jnp.where(causal[None], ws, 0)
    mix = jnp.einsum('gts,bcsgd->bctgd', wm, vp) + bs.T[:, :, None]
    mix = mix.reshape(b, nc * GM_CHUNK, GM_DIM)[:, :t]
    return u * mix, v


def moe_ffn(h, w_router, b_router, w1, b1, w2, b2):
    shape = h.shape
    xt = h.reshape(-1, D_MODEL)
    n = xt.shape[0]
    logits = jnp.dot(xt, w_router, preferred_element_type=jnp.float32) + b_router.astype(jnp.float32)
    top_v, top_i = lax.top_k(logits, TOP_K)
    top_w = jax.nn.softmax(top_v, axis=-1)
    n_assign = n * TOP_K
    per_expert = max(1, n_assign // N_EXPERTS)
    rb = min(MOE_BLOCK, max(8, 1 << (per_expert - 1).bit_length()))
    n_blocks = -(-(n_assign + N_EXPERTS * (rb - 1)) // rb)
    cap = n_blocks * rb
    flat_e = top_i.reshape(-1)
    order = jnp.argsort(flat_e)
    se = flat_e[order]
    st = (order // TOP_K).astype(jnp.int32)
    sw = top_w.reshape(-1)[order]
    counts = jnp.bincount(flat_e, length=N_EXPERTS)
    start = jnp.cumsum(counts) - counts
    padded = (counts + rb - 1) // rb * rb
    pad_end = jnp.cumsum(padded)
    dest = pad_end[se] - padded[se] + jnp.arange(n_assign) - start[se]
    row_tok = jnp.full((cap,), n, jnp.int32).at[dest].set(st)
    row_w = jnp.zeros((cap,), jnp.float32).at[dest].set(sw)
    block_e = jnp.minimum(jnp.searchsorted(pad_end, jnp.arange(n_blocks) * rb, side='right'), N_EXPERTS - 1)
    x_rows = jnp.concatenate([xt, jnp.zeros((1, D_MODEL), xt.dtype)], axis=0)

    def expert_block(args):
        tok, wgt, e = args
        hh = jnp.dot(x_rows[tok], w1[e]) + b1[e]
        glu, lin = jnp.split(hh, 2, axis=-1)
        glu = jnp.minimum(glu, SWIGLU_LIMIT)
        lin = jnp.clip(lin, -SWIGLU_LIMIT, SWIGLU_LIMIT)
        act = glu * jax.nn.sigmoid(SWIGLU_ALPHA * glu) * (lin + 1)
        out = jnp.dot(act, w2[e]) + b2[e]
        return out.astype(jnp.float32) * wgt[:, None]

    rows = lax.map(expert_block, (row_tok.reshape(n_blocks, rb), row_w.reshape(n_blocks, rb), block_e))
    y = jnp.zeros((n + 1, D_MODEL), jnp.float32).at[row_tok].add(rows.reshape(cap, D_MODEL))[:n]
    return y.astype(h.dtype).reshape(shape)


def run_layer(x, c, pos, past, conv_prev, ssm_prev, W):
    (ln1_g, ln1_b, ln2_g, ln2_b, w_ada, b_ada, w_in, idx_kn_g, idx_kn_b,
     conv_w, conv_b, dt_bias, a_log, d_skip, ssm_norm_g,
     gm_ln_g, gm_ln_b, gm_ws, gm_bs, w_gate, w_branch, w_out,
     w_router, b_router, w1, b1, w2, b2) = W
    b, t, _ = x.shape
    mod = jnp.dot(jax.nn.silu(c), w_ada) + b_ada
    sh1, sc1, g1, sh2, sc2, g2 = jnp.split(mod[:, None, :], 6, axis=-1)
    h = x * (1 + sc1) + sh1
    q, k, v, qi, ki, wi, z, xbc, dt_raw, gm = jnp.split(jnp.dot(h, w_in), _offsets(IN_SPLITS), axis=-1)
    q = rope(q.reshape(b, t, N_HEADS, HEAD_DIM), pos, ROT_DIM).reshape(b, t, N_KV_HEADS, Q_PER_KV, HEAD_DIM)
    k = rope(k.reshape(b, t, N_KV_HEADS, HEAD_DIM), pos, ROT_DIM)
    v = v.reshape(b, t, N_KV_HEADS, HEAD_DIM)
    qi = rope(qi.reshape(b, t, IDX_HEADS, IDX_DIM), pos, IDX_ROT)
    ki = rope(layer_norm(ki, idx_kn_g, idx_kn_b)[:, :, None, :], pos, IDX_ROT)[:, :, 0, :]
    if past is None:
        k_all, v_all, ki_all = k, v, ki
    else:
        k_all = jnp.concatenate([past[0].astype(k.dtype), k], axis=1)
        v_all = jnp.concatenate([past[1].astype(v.dtype), v], axis=1)
        ki_all = jnp.concatenate([past[2].astype(ki.dtype), ki], axis=1)
    o_a = dsa_attention(q, qi, wi, pos, k_all, v_all, ki_all)
    o_b, ssm_new, conv_new = ssd_mixer(z, xbc, dt_raw, conv_prev, conv_w, conv_b, dt_bias, a_log,
                                       d_skip, ssm_norm_g, ssm_prev)
    o_c, gm_v = gmlp_mixer(jax.nn.gelu(gm), gm_ln_g, gm_ln_b, gm_ws, gm_bs)
    ga, gb, gc = jnp.split(jax.nn.sigmoid(jnp.dot(h, w_gate)), N_BRANCH, axis=-1)
    merged = (ga * jnp.dot(o_a, w_branch[0]) + gb * jnp.dot(o_b, w_branch[1])
              + gc * jnp.dot(o_c, w_branch[2]))
    x = layer_norm(DN_ALPHA * x + g1 * jnp.dot(merged, w_out), ln1_g, ln1_b)
    h2 = x * (1 + sc2) + sh2
    x = layer_norm(DN_ALPHA * x + g2 * moe_ffn(h2, w_router, b_router, w1, b1, w2, b2), ln2_g, ln2_b)
    return x, (k, v, ki, ssm_new, conv_new, gm_v)


def setup_inputs(seed: int = 0) -> dict:
    key = jax.random.key(seed)
    keys = list(jax.random.split(key, 48))
    f32 = jnp.float32

    def nrm(shape, scale):
        return jax.random.normal(keys.pop(), shape, f32) * scale

    def gain(shape):
        return 1.0 + nrm(shape, 0.02)

    L, D = DEPTH, D_MODEL
    n_pages = PAST_LEN // PAGE_SIZE
    n_used = DEC_BATCH * n_pages
    n_pool = n_used + max(1, n_used // 4)
    page_table = jax.random.permutation(keys.pop(), n_pool)[:n_used].reshape(DEC_BATCH, n_pages).astype(jnp.int32)
    dt0 = jnp.exp(jax.random.uniform(keys.pop(), (L, SSM_HEADS), f32, math.log(1e-3), math.log(1e-1)))
    dt_bias = dt0 + jnp.log(-jnp.expm1(-dt0))
    a_log = jnp.log(jax.random.uniform(keys.pop(), (L, SSM_HEADS), f32, 1.0, 16.0))
    return {
        'x_prompt': nrm((BATCH, SEQ, D), 1.0),
        'x_sample': nrm((DEC_BATCH, DEC_SEQ, D), 1.0),
        'cache_k': nrm((n_pool, L, PAGE_SIZE, N_KV_HEADS, HEAD_DIM), 1.0),
        'cache_v': nrm((n_pool, L, PAGE_SIZE, N_KV_HEADS, HEAD_DIM), 1.0),
        'cache_idx_k': nrm((n_pool, L, PAGE_SIZE, IDX_DIM), 1.0),
        'state_ssm': nrm((DEC_BATCH, L, SSM_HEADS, SSM_HEAD_DIM, SSM_STATE), 0.5),
        'state_conv': nrm((DEC_BATCH, L, CONV_W - 1, XBC_DIM), 1.0),
        'page_table': page_table,
        'c_prompt': nrm((BATCH, D), 1.0),
        'c_sample': nrm((DEC_BATCH, D), 1.0),
        'ln1_g': gain((L, D)),
        'ln1_b': nrm((L, D), 0.02),
        'ln2_g': gain((L, D)),
        'ln2_b': nrm((L, D), 0.02),
        'w_ada': nrm((L, D, 6 * D), D ** -0.5),
        'b_ada': nrm((L, 6 * D), 0.02),
        'w_in': nrm((L, D, IN_COLS), D ** -0.5),
        'idx_kn_g': gain((L, IDX_DIM)),
        'idx_kn_b': nrm((L, IDX_DIM), 0.02),
        'conv_w': nrm((L, CONV_W, XBC_DIM), CONV_W ** -0.5),
        'conv_b': nrm((L, XBC_DIM), 0.02),
        'dt_bias': dt_bias,
        'a_log': a_log,
        'd_skip': gain((L, SSM_HEADS)),
        'ssm_norm_g': gain((L, SSM_DIM)),
        'gm_ln_g': gain((L, GM_DIM)),
        'gm_ln_b': nrm((L, GM_DIM), 0.02),
        'gm_ws': nrm((L, GM_GROUPS, GM_CHUNK, GM_CHUNK), GM_CHUNK ** -0.5),
        'gm_bs': gain((L, GM_GROUPS, GM_CHUNK)),
        'w_gate': nrm((L, D, N_BRANCH * D), D ** -0.5),
        'w_branch': nrm((L, N_BRANCH, BRANCH_DIM, D), BRANCH_DIM ** -0.5),
        'w_out': nrm((L, D, D), D ** -0.5 * DN_BETA),
        'w_router': nrm((L, D, N_EXPERTS), D ** -0.5),
        'b_router': nrm((L, N_EXPERTS), 0.01),
        'w1': nrm((L, N_EXPERTS, D, 2 * EXPERT_DIM), D ** -0.5),
        'b1': nrm((L, N_EXPERTS, 2 * EXPERT_DIM), 0.02),
        'w2': nrm((L, N_EXPERTS, EXPERT_DIM, D), EXPERT_DIM ** -0.5 * DN_BETA),
        'b2': nrm((L, N_EXPERTS, D), 0.02),
    }


def reference(x_prompt, x_sample, cache_k, cache_v, cache_idx_k, state_ssm, state_conv, page_table,
              c_prompt, c_sample, ln1_g, ln1_b, ln2_g, ln2_b, w_ada, b_ada, w_in, idx_kn_g, idx_kn_b,
              conv_w, conv_b, dt_bias, a_log, d_skip, ssm_norm_g, gm_ln_g, gm_ln_b, gm_ws, gm_bs,
              w_gate, w_branch, w_out, w_router, b_router, w1, b1, w2, b2):
    bp, sp, _ = x_prompt.shape
    db, ds, _ = x_sample.shape
    past_len = page_table.shape[1] * PAGE_SIZE
    pos_p = jnp.arange(sp, dtype=jnp.int32)
    pos_s = past_len + jnp.arange(ds, dtype=jnp.int32)
    conv0 = jnp.zeros((bp, CONV_W - 1, XBC_DIM), x_prompt.dtype)
    ssm0 = jnp.zeros((bp, SSM_HEADS, SSM_HEAD_DIM, SSM_STATE), x_prompt.dtype)
    xp, xs = x_prompt, x_sample
    pk, pv, pki, pssm, pconv = [], [], [], [], []
    sk, sv, ski, sssm, sconv, sgm = [], [], [], [], [], []
    for l in range(DEPTH):
        W = (ln1_g[l], ln1_b[l], ln2_g[l], ln2_b[l], w_ada[l], b_ada[l], w_in[l], idx_kn_g[l], idx_kn_b[l],
             conv_w[l], conv_b[l], dt_bias[l], a_log[l], d_skip[l], ssm_norm_g[l],
             gm_ln_g[l], gm_ln_b[l], gm_ws[l], gm_bs[l], w_gate[l], w_branch[l], w_out[l],
             w_router[l], b_router[l], w1[l], b1[l], w2[l], b2[l])
        xp, (k_, v_, ki_, h_, cv_, _) = run_layer(xp, c_prompt, pos_p, None, conv0, ssm0, W)
        pk.append(k_)
        pv.append(v_)
        pki.append(ki_)
        pssm.append(h_)
        pconv.append(cv_)
        past = (cache_k[page_table, l].reshape(db, past_len, N_KV_HEADS, HEAD_DIM),
                cache_v[page_table, l].reshape(db, past_len, N_KV_HEADS, HEAD_DIM),
                cache_idx_k[page_table, l].reshape(db, past_len, IDX_DIM))
        xs, (k_, v_, ki_, h_, cv_, g_) = run_layer(xs, c_sample, pos_s, past, state_conv[:, l], state_ssm[:, l], W)
        sk.append(k_)
        sv.append(v_)
        ski.append(ki_)
        sssm.append(h_)
        sconv.append(cv_)
        sgm.append(g_)
    return (xp, xs,
            jnp.stack(pk, axis=1), jnp.stack(pv, axis=1), jnp.stack(pki, axis=1),
            jnp.stack(pssm, axis=1), jnp.stack(pconv, axis=1),
            jnp.stack(sk, axis=1), jnp.stack(sv, axis=1), jnp.stack(ski, axis=1),
            jnp.stack(sssm, axis=1), jnp.stack(sconv, axis=1), jnp.stack(sgm, axis=1))
```

```python
import functools
import math

import jax
import jax.numpy as jnp
import numpy as np
from jax import lax
from jax.experimental import pallas as pl
from jax.experimental.pallas import tpu as pltpu

F32 = jnp.float32
BF16 = jnp.bfloat16
I32 = jnp.int32

D_MODEL = 4096
PAGE_SIZE = 128
N_HEADS = 16
HEAD_DIM = 128
N_KV_HEADS = 4
Q_PER_KV = N_HEADS // N_KV_HEADS
ATT_DIM = N_HEADS * HEAD_DIM
KV_DIM = N_KV_HEADS * HEAD_DIM
ROT_DIM = HEAD_DIM // 4
IDX_HEADS = 8
IDX_DIM = 128
TOPK_MAX = 256
ROPE_THETA = 500000.0
ATT_SCALE = HEAD_DIM ** -0.5
IDX_SCALE = (IDX_DIM * IDX_HEADS) ** -0.5
SSM_DIM = 2048
SSM_HEAD_DIM = 64
SSM_HEADS = SSM_DIM // SSM_HEAD_DIM
SSM_GROUPS = 4
SSM_STATE = 128
CONV_W = 4
XBC_DIM = SSM_DIM + 2 * SSM_GROUPS * SSM_STATE
GM_DIM = 2048
GM_GROUPS = 16
N_BRANCH = 3
BRANCH_DIM = 2048
N_EXPERTS = 32
TOP_K = 4
EXPERT_DIM = 1024
SWIGLU_ALPHA = 1.702
SWIGLU_LIMIT = 7.0
LN_EPS = 1e-5
RMS_EPS = 1e-5

CHUNK = 128
SAMPLE_ROWS = 8
LANES = 128
COL_Z = 0
COL_GM = COL_Z + SSM_DIM
COL_XBC = COL_GM + 2 * GM_DIM
COL_ATT = COL_XBC + XBC_DIM
ATT_COLS = ATT_DIM + 2 * KV_DIM + IDX_HEADS * IDX_DIM + IDX_DIM + LANES
PROJ_COLS = COL_ATT + ATT_COLS
ATT_K_BLK = ATT_DIM // KV_DIM
ATT_V_BLK = ATT_K_BLK + 1
ATT_QI_BLK = (ATT_DIM + 2 * KV_DIM) // (IDX_HEADS * IDX_DIM)
ATT_KI_BLK = (ATT_DIM + 2 * KV_DIM + IDX_HEADS * IDX_DIM) // LANES
ATT_DTWI_BLK = ATT_KI_BLK + 1
WI_LANE = SSM_HEADS
MOE_TM = 256
MOE_TF = 512
VMEM_LIMIT = 56 * 1024 * 1024


def _cp(n_axes, vmem=VMEM_LIMIT):
    return pltpu.CompilerParams(dimension_semantics=("arbitrary",) * n_axes, vmem_limit_bytes=vmem)


def _dot(a, b):
    return jnp.dot(a, b, preferred_element_type=F32)


def _dot_nt(a, b):
    return lax.dot_general(a, b, (((1,), (1,)), ((), ())), preferred_element_type=F32)


def _split3(x):
    hi = x.astype(BF16)
    r1 = x - hi.astype(F32)
    mid = r1.astype(BF16)
    lo = (r1 - mid.astype(F32)).astype(BF16)
    return hi, mid, lo


def _dot3_left(m_bf16, x):
    hi, mid, lo = _split3(x)
    return _dot(m_bf16, hi) + _dot(m_bf16, mid) + _dot(m_bf16, lo)


def _dot3_right(x, m_bf16):
    hi, mid, lo = _split3(x)
    return _dot(hi, m_bf16) + _dot(mid, m_bf16) + _dot(lo, m_bf16)


def _silu(x):
    return x * jax.nn.sigmoid(x)


def _rope(x, c, s1, s2):
    half = ROT_DIM // 2
    return x * c + pltpu.roll(x, LANES - half, 1) * s1 + pltpu.roll(x, half, 1) * s2


def _rope_tables(pos):
    half = ROT_DIM // 2
    inv = ROPE_THETA ** (-jnp.arange(half, dtype=F32) / half)
    ang = pos.astype(F32)[:, None] * inv
    cos, sin = jnp.cos(ang), jnp.sin(ang)
    n = pos.shape[0]
    rest = LANES - ROT_DIM
    c = jnp.concatenate([cos, cos, jnp.ones((n, rest), F32)], axis=1)
    s1 = jnp.concatenate([-sin, jnp.zeros((n, LANES - half), F32)], axis=1)
    s2 = jnp.concatenate([jnp.zeros((n, half), F32), sin, jnp.zeros((n, rest), F32)], axis=1)
    return c, s1, s2


def _mod_body(c_ref, w_ref, b_ref, o_ref):
    s = _silu(c_ref[...]).astype(BF16)
    o_ref[0] = _dot(s, w_ref[0].astype(BF16)) + b_ref[0]


def _mod_call(c_rows, w_ada, b_ada):
    n_layers, d, n = w_ada.shape
    r = c_rows.shape[0]
    tn = 512
    return pl.pallas_call(
        _mod_body,
        grid=(n_layers, n // tn),
        in_specs=[pl.BlockSpec((r, d), lambda l, j: (0, 0)),
                  pl.BlockSpec((1, d, tn), lambda l, j: (l, 0, j)),
                  pl.BlockSpec((1, 1, tn), lambda l, j: (l, 0, j))],
        out_specs=pl.BlockSpec((1, r, tn), lambda l, j: (l, 0, j)),
        out_shape=jax.ShapeDtypeStruct((n_layers, r, n), F32),
        compiler_params=_cp(2),
    )(c_rows, w_ada, b_ada.reshape(n_layers, 1, n))


class _Group:
    def __init__(self, n, tm, rows_per_seq, per_row):
        self.n, self.tm, self.rows_per_seq, self.per_row = n, tm, rows_per_seq, per_row

    def mod_array(self, m):
        if self.per_row:
            return jnp.repeat(m, self.rows_per_seq, axis=0)[None]
        return m[:, None, :]

    def mod_spec(self, width=D_MODEL):
        if self.per_row:
            return pl.BlockSpec((1, self.n, width), lambda i, *_: (0, 0, 0))
        tm, rps = self.tm, self.rows_per_seq
        return pl.BlockSpec((1, 1, width), lambda i, *_: ((i * tm) // rps, 0, 0))


def _modulate_body(x_ref, sc_ref, sh_ref, h_ref):
    h_ref[...] = (x_ref[...] * (1.0 + sc_ref[0]) + sh_ref[0]).astype(BF16)


def _modulate_call(grp, x, sc, sh):
    n, d = x.shape
    tm = grp.tm
    return pl.pallas_call(
        _modulate_body,
        grid=(n // tm,),
        in_specs=[pl.BlockSpec((tm, d), lambda i: (i, 0)), grp.mod_spec(), grp.mod_spec()],
        out_specs=pl.BlockSpec((tm, d), lambda i: (i, 0)),
        out_shape=jax.ShapeDtypeStruct((n, d), BF16),
        compiler_params=_cp(1),
    )(x, sc, sh)


def _mm_body(a_ref, w_ref, o_ref):
    o_ref[...] = _dot(a_ref[...], w_ref[0]).astype(o_ref.dtype)


def _mm_call(a, w, layer, col0, ncols, tm, tn, out_dtype=F32):
    n, k = a.shape
    cb = col0 // tn
    return pl.pallas_call(
        _mm_body,
        grid=(n // tm, ncols // tn),
        in_specs=[pl.BlockSpec((tm, k), lambda i, j: (i, 0)),
                  pl.BlockSpec((1, k, tn), lambda i, j: (layer, 0, cb + j))],
        out_specs=pl.BlockSpec((tm, tn), lambda i, j: (i, j)),
        out_shape=jax.ShapeDtypeStruct((n, ncols), out_dtype),
        compiler_params=_cp(2),
    )(a, w)


def _kvpost_body(k_ref, v_ref, ki_ref, c_ref, s1_ref, s2_ref, g_ref, b_ref,
                 ko_ref, kio_ref, kb_ref, vb_ref, kib_ref):
    c, s1, s2 = c_ref[...], s1_ref[...], s2_ref[...]
    for h in range(N_KV_HEADS):
        sl = slice(h * HEAD_DIM, (h + 1) * HEAD_DIM)
        kh = _rope(k_ref[:, sl], c, s1, s2)
        ko_ref[:, sl] = kh
        kb_ref[:, sl] = kh.astype(BF16)
    vb_ref[...] = v_ref[...].astype(BF16)
    ki = ki_ref[...]
    mu = jnp.mean(ki, axis=-1, keepdims=True)
    var = jnp.mean(jnp.square(ki - mu), axis=-1, keepdims=True)
    kin = (ki - mu) * lax.rsqrt(var + LN_EPS) * g_ref[0] + b_ref[0]
    kir = _rope(kin, c, s1, s2)
    kio_ref[...] = kir
    kib_ref[...] = kir.astype(BF16)


def _kvpost_call(att, tables, tab_blocks, g, b, layer, tm):
    n = att.shape[0]
    tab = pl.BlockSpec((tm, LANES), lambda i: (i % tab_blocks, 0))
    vec = pl.BlockSpec((1, 1, LANES), lambda i: (layer, 0, 0))
    row = lambda w: pl.BlockSpec((tm, w), lambda i: (i, 0))
    return pl.pallas_call(
        _kvpost_body,
        grid=(n // tm,),
        in_specs=[pl.BlockSpec((tm, KV_DIM), lambda i: (i, ATT_K_BLK)),
                  pl.BlockSpec((tm, KV_DIM), lambda i: (i, ATT_V_BLK)),
                  pl.BlockSpec((tm, IDX_DIM), lambda i: (i, ATT_KI_BLK)),
                  tab, tab, tab, vec, vec],
        out_specs=[row(KV_DIM), row(IDX_DIM), row(KV_DIM), row(KV_DIM), row(IDX_DIM)],
        out_shape=[jax.ShapeDtypeStruct((n, KV_DIM), F32), jax.ShapeDtypeStruct((n, IDX_DIM), F32),
                   jax.ShapeDtypeStruct((n, KV_DIM), BF16), jax.ShapeDtypeStruct((n, KV_DIM), BF16),
                   jax.ShapeDtypeStruct((n, IDX_DIM), BF16)],
        compiler_params=_cp(1),
    )(att, att, att, *tables, g, b)


def _pages_body(pt_ref, ck_ref, cv_ref, ci_ref, kn_ref, vn_ref, in_ref, ko_ref, vo_ref, io_ref, *, n_pages):
    p = pl.program_id(1)

    @pl.when(p < n_pages)
    def _():
        for h in range(N_KV_HEADS):
            sl = slice(h * HEAD_DIM, (h + 1) * HEAD_DIM)
            ko_ref[0, :, sl] = ck_ref[:, h, :].astype(BF16)
            vo_ref[0, :, sl] = cv_ref[:, h, :].astype(BF16)
        io_ref[0] = ci_ref[...].astype(BF16)

    @pl.when(p == n_pages)
    def _():
        r = kn_ref.shape[1]
        zk = jnp.zeros((PAGE_SIZE - r, KV_DIM), F32)
        zi = jnp.zeros((PAGE_SIZE - r, IDX_DIM), F32)
        ko_ref[0] = jnp.concatenate([kn_ref[0], zk], axis=0).astype(BF16)
        vo_ref[0] = jnp.concatenate([vn_ref[0], zk], axis=0).astype(BF16)
        io_ref[0] = jnp.concatenate([in_ref[0], zi], axis=0).astype(BF16)


def _pages_call(page_table, cache_k, cache_v, cache_idx_k, k_new, v_new, ki_new, layer):
    nb, n_pages = page_table.shape
    r = k_new.shape[1]
    nk = (n_pages + 1) * PAGE_SIZE

    def page(b, p, pt):
        return pt[b * n_pages + jnp.minimum(p, n_pages - 1)]

    grid_spec = pltpu.PrefetchScalarGridSpec(
        num_scalar_prefetch=1,
        grid=(nb, n_pages + 1),
        in_specs=[pl.BlockSpec((None, None, PAGE_SIZE, N_KV_HEADS, HEAD_DIM), lambda b, p, pt: (page(b, p, pt), layer, 0, 0, 0)),
                  pl.BlockSpec((None, None, PAGE_SIZE, N_KV_HEADS, HEAD_DIM), lambda b, p, pt: (page(b, p, pt), layer, 0, 0, 0)),
                  pl.BlockSpec((None, None, PAGE_SIZE, IDX_DIM), lambda b, p, pt: (page(b, p, pt), layer, 0, 0)),
                  pl.BlockSpec((1, r, KV_DIM), lambda b, p, pt: (b, 0, 0)),
                  pl.BlockSpec((1, r, KV_DIM), lambda b, p, pt: (b, 0, 0)),
                  pl.BlockSpec((1, r, IDX_DIM), lambda b, p, pt: (b, 0, 0))],
        out_specs=[pl.BlockSpec((1, PAGE_SIZE, KV_DIM), lambda b, p, pt: (b, p, 0)),
                   pl.BlockSpec((1, PAGE_SIZE, KV_DIM), lambda b, p, pt: (b, p, 0)),
                   pl.BlockSpec((1, PAGE_SIZE, IDX_DIM), lambda b, p, pt: (b, p, 0))],
    )
    return pl.pallas_call(
        functools.partial(_pages_body, n_pages=n_pages),
        grid_spec=grid_spec,
        out_shape=[jax.ShapeDtypeStruct((nb, nk, KV_DIM), BF16), jax.ShapeDtypeStruct((nb, nk, KV_DIM), BF16),
                   jax.ShapeDtypeStruct((nb, nk, IDX_DIM), BF16)],
        compiler_params=_cp(2),
    )(page_table.reshape(-1), cache_k, cache_v, cache_idx_k, k_new, v_new, ki_new)


def _sort_key(score):
    bits = lax.bitcast_convert_type(score, I32)
    return jnp.where(bits < 0, bits ^ jnp.int32(0x7FFFFFFF), bits)


def _kth_largest(key, k):
    kf = jnp.float32(k)

    def count_ge(c):
        return jnp.sum((key >= c).astype(F32), axis=1, keepdims=True)

    int_min = jnp.int32(-2 ** 31)
    prefix = jnp.where(count_ge(jnp.int32(0)) >= kf, jnp.int32(0), int_min)

    def body(i, prefix):
        cand = prefix | lax.shift_left(jnp.int32(1), jnp.int32(30) - i)
        return jnp.where(count_ge(cand) >= kf, cand, prefix)

    return lax.fori_loop(0, 31, body, prefix)


def _select_topk(key, kpos, k):
    thr = _kth_largest(key, k)
    gt = key > thr
    eq = key == thr
    need = jnp.float32(k) - jnp.sum(gt.astype(F32), axis=1, keepdims=True)
    nbits = int(math.ceil(math.log2(key.shape[1] + 1)))

    def body(i, bound):
        cand = bound | lax.shift_left(jnp.int32(1), jnp.int32(nbits - 1) - i)
        cnt = jnp.sum((eq & (kpos < cand)).astype(F32), axis=1, keepdims=True)
        return jnp.where(cnt <= need, cand, bound)

    bound = lax.fori_loop(0, nbits, body, jnp.zeros_like(thr))
    return gt | (eq & (kpos < bound))


def _dsa_body(q_ref, qi_ref, dtwi_ref, c_ref, s1_ref, s2_ref, k_ref, v_ref, ki_ref, o_ref, *, q_pos0, topk):
    tq = q_ref.shape[1]
    nk = k_ref.shape[1]
    c, s1, s2 = c_ref[...], s1_ref[...], s2_ref[...]
    ki = ki_ref[0]
    dtwi = dtwi_ref[0]
    score = jnp.zeros((tq, nk), F32)
    for h in range(IDX_HEADS):
        qh = _rope(qi_ref[0, :, h * IDX_DIM:(h + 1) * IDX_DIM], c, s1, s2).astype(BF16)
        d = _dot_nt(qh, ki)
        score = score + jnp.maximum(d, 0.0) * dtwi[:, WI_LANE + h:WI_LANE + h + 1]
    score = score * IDX_SCALE
    kpos = lax.broadcasted_iota(I32, (tq, nk), 1)
    qpos = q_pos0 + pl.program_id(1) * tq + lax.broadcasted_iota(I32, (tq, nk), 0)
    visible = kpos <= qpos
    score = jnp.where(visible, score, -jnp.inf)
    sel = _select_topk(_sort_key(score), kpos, topk) & visible
    bias = jnp.where(sel, 0.0, -jnp.inf)[None]
    for kvh in range(N_KV_HEADS):
        heads = [kvh * Q_PER_KV + g for g in range(Q_PER_KV)]
        qh = jnp.concatenate(
            [_rope(q_ref[0, :, h * HEAD_DIM:(h + 1) * HEAD_DIM], c, s1, s2) for h in heads], axis=0).astype(BF16)
        sl = slice(kvh * HEAD_DIM, (kvh + 1) * HEAD_DIM)
        logits = _dot_nt(qh, k_ref[0, :, sl]) * ATT_SCALE
        logits = logits.reshape(Q_PER_KV, tq, nk) + bias
        m = jnp.max(logits, axis=-1, keepdims=True)
        p = jnp.exp(logits - m)
        denom = jnp.sum(p, axis=-1, keepdims=True)
        o = _dot(p.astype(BF16).reshape(Q_PER_KV * tq, nk), v_ref[0, :, sl])
        o = o.reshape(Q_PER_KV, tq, HEAD_DIM) / denom
        for g, h in enumerate(heads):
            o_ref[0, :, h * HEAD_DIM:(h + 1) * HEAD_DIM] = o[g].astype(o_ref.dtype)


def _dsa_call(att3, tables, kb, vb, kib, tq, q_pos0, topk, out_dtype):
    nb, tq_total, _ = att3.shape
    nk = kb.shape[1]
    tab = pl.BlockSpec((tq, LANES), lambda b, j: (j, 0))
    return pl.pallas_call(
        functools.partial(_dsa_body, q_pos0=q_pos0, topk=topk),
        grid=(nb, tq_total // tq),
        in_specs=[pl.BlockSpec((1, tq, ATT_DIM), lambda b, j: (b, j, 0)),
                  pl.BlockSpec((1, tq, IDX_HEADS * IDX_DIM), lambda b, j: (b, j, ATT_QI_BLK)),
                  pl.BlockSpec((1, tq, LANES), lambda b, j: (b, j, ATT_DTWI_BLK)),
                  tab, tab, tab,
                  pl.BlockSpec((1, nk, KV_DIM), lambda b, j: (b, 0, 0)),
                  pl.BlockSpec((1, nk, KV_DIM), lambda b, j: (b, 0, 0)),
                  pl.BlockSpec((1, nk, IDX_DIM), lambda b, j: (b, 0, 0))],
        out_specs=pl.BlockSpec((1, tq, ATT_DIM), lambda b, j: (b, j, 0)),
        out_shape=jax.ShapeDtypeStruct((nb, tq_total, ATT_DIM), out_dtype),
        compiler_params=_cp(2),
    )(att3, att3, att3, *tables, kb, vb, kib)


def _pad_rows(x, rows):
    r = x.shape[0]
    if r == rows:
        return x
    return jnp.concatenate([x, jnp.zeros((rows - r,) + x.shape[1:], x.dtype)], axis=0)


def _ssd_body(z_ref, xbc_ref, dtwi_ref, cprev_ref, h0_ref, cw_ref, cb_ref, dtb_ref, alog_ref, dskip_ref, ng_ref,
              y_ref, hout_ref, cnew_ref, xp_ref, ht_ref, *, valid_last, n_chunks):
    ci = pl.program_id(1)
    rows = xbc_ref.shape[1]
    e_heads = SSM_HEADS // SSM_GROUPS
    gw = e_heads * SSM_HEAD_DIM
    pre = 8

    @pl.when(ci == 0)
    def _():
        xp_ref[0:pre, :] = _pad_rows(jnp.concatenate(
            [jnp.zeros((pre - (CONV_W - 1), XBC_DIM), F32), cprev_ref[0]], axis=0), pre)
        for g in range(SSM_GROUPS):
            ht_ref[g] = h0_ref[0, g * gw:(g + 1) * gw, :].T

    xp_ref[pre:pre + CHUNK, :] = _pad_rows(xbc_ref[0], CHUNK)
    conv = cb_ref[0]
    for i in range(CONV_W):
        off = pre - (CONV_W - 1) + i
        conv = conv + xp_ref[off:off + CHUNK, :] * cw_ref[0, i:i + 1, :]
    last = valid_last if n_chunks == 1 else CHUNK

    @pl.when(ci == n_chunks - 1)
    def _():
        cnew_ref[0] = xp_ref[pre + valid_last - (CONV_W - 1):pre + valid_last, :]

    if n_chunks > 1:
        tail = xp_ref[pre + last - (CONV_W - 1):pre + last, :]
        xp_ref[pre - (CONV_W - 1):pre, :] = tail

    u = _silu(conv)
    xs = u[:, :SSM_DIM]
    bm = u[:, SSM_DIM:SSM_DIM + SSM_GROUPS * SSM_STATE]
    cm = u[:, SSM_DIM + SSM_GROUPS * SSM_STATE:]

    dt_raw = _pad_rows(dtwi_ref[0], CHUNK)
    lane = lax.broadcasted_iota(I32, (CHUNK, LANES), 1)
    rowi = lax.broadcasted_iota(I32, (CHUNK, LANES), 0)
    xdt_in = dt_raw + dtb_ref[0]
    dt = jnp.maximum(xdt_in, 0.0) + jnp.log1p(jnp.exp(-jnp.abs(xdt_in)))
    row_ok = rowi < jnp.where(ci == n_chunks - 1, valid_last, CHUNK)
    dt = jnp.where((lane < SSM_HEADS) & row_ok, dt, 0.0)
    a = -jnp.exp(alog_ref[0])
    da = dt * a
    tri = (lax.broadcasted_iota(I32, (CHUNK, CHUNK), 1) <= lax.broadcasted_iota(I32, (CHUNK, CHUNK), 0))
    cum = _dot3_left(tri.astype(BF16), da)
    cum_t = cum.T
    hh = lax.broadcasted_iota(I32, (LANES, SSM_DIM), 0)
    cc = lax.broadcasted_iota(I32, (LANES, SSM_DIM), 1)
    expand = (cc // SSM_HEAD_DIM == hh).astype(BF16)
    dt_x = _dot3_right(dt, expand)
    cum_x = _dot3_right(cum, expand)
    cum_last = cum[CHUNK - 1:CHUNK, :]
    cum_last_x = cum_x[CHUNK - 1:CHUNK, :]
    xdt = xs * dt_x
    grow = jnp.exp(cum_x)
    xdt_end = (xdt * jnp.exp(cum_last_x - cum_x)).astype(BF16)
    xdt_b = xdt.astype(BF16)
    lane_lo = lax.broadcasted_iota(I32, (CHUNK, LANES), 1) < SSM_HEAD_DIM

    for g in range(SSM_GROUPS):
        st = slice(g * SSM_STATE, (g + 1) * SSM_STATE)
        ch = slice(g * gw, (g + 1) * gw)
        bm_g = bm[:, st].astype(BF16)
        cm_g = cm[:, st].astype(BF16)
        cb = _dot_nt(cm_g, bm_g)
        h_prev = ht_ref[g]
        y_off = _dot(cm_g, h_prev.astype(BF16)) * grow[:, ch]
        parts = []
        for pr in range(e_heads // 2):
            acc = None
            pair = slice(g * gw + pr * LANES, g * gw + (pr + 1) * LANES)
            xp_pair = xdt_b[:, pair]
            for half in range(2):
                h = g * e_heads + pr * 2 + half
                seg = cum[:, h:h + 1] - cum_t[h:h + 1, :]
                dec = jnp.exp(jnp.where(tri, seg, -jnp.inf))
                m = (cb * dec).astype(BF16)
                keep = lane_lo if half == 0 else jnp.logical_not(lane_lo)
                part = _dot(m, jnp.where(keep, xp_pair, jnp.zeros_like(xp_pair)))
                acc = part if acc is None else acc + part
            parts.append(acc)
        y_g = jnp.concatenate(parts, axis=1) + y_off + dskip_ref[0, :, ch] * xs[:, ch]
        y_g = y_g * _silu(z_pad_slice(z_ref, ch))
        ms = jnp.mean(jnp.square(y_g), axis=-1, keepdims=True)
        y_g = y_g * lax.rsqrt(ms + RMS_EPS) * ng_ref[0, :, ch]
        y_ref[0, :, ch] = y_g[:rows].astype(y_ref.dtype)
        states_t = _dot(bm_g.astype(F32).T.astype(BF16), xdt_end[:, ch])
        ht_ref[g] = h_prev * jnp.exp(cum_last_x[:, ch]) + states_t

    @pl.when(ci == n_chunks - 1)
    def _():
        for g in range(SSM_GROUPS):
            hout_ref[0, g * gw:(g + 1) * gw, :] = ht_ref[g].T


def z_pad_slice(z_ref, ch):
    return _pad_rows(z_ref[0, :, ch], CHUNK)


def _ssd_call(z3, xbc3, att3, conv_prev, h0, layer, params, valid_last, out_dtype, h0_layered):
    nb, t, _ = z3.shape
    rows = min(t, CHUNK)
    n_chunks = t // rows
    conv_w, conv_b, dt_bias, a_log, d_skip, norm_g = params
    hp = SSM_HEADS * SSM_HEAD_DIM
    if h0_layered:
        cprev_spec = pl.BlockSpec((1, None, CONV_W - 1, XBC_DIM), lambda b, c: (b, layer, 0, 0))
        h0_spec = pl.BlockSpec((1, None, hp, SSM_STATE), lambda b, c: (b, layer, 0, 0))
    else:
        cprev_spec = pl.BlockSpec((1, CONV_W - 1, XBC_DIM), lambda b, c: (b, 0, 0))
        h0_spec = pl.BlockSpec((1, hp, SSM_STATE), lambda b, c: (b, 0, 0))
    lvec = lambda w: pl.BlockSpec((1, 1, w), lambda b, c: (layer, 0, 0))
    return pl.pallas_call(
        functools.partial(_ssd_body, valid_last=valid_last, n_chunks=n_chunks),
        grid=(nb, n_chunks),
        in_specs=[pl.BlockSpec((1, rows, SSM_DIM), lambda b, c: (b, c, 0)),
                  pl.BlockSpec((1, rows, XBC_DIM), lambda b, c: (b, c, 0)),
                  pl.BlockSpec((1, rows, LANES), lambda b, c: (b, c, ATT_DTWI_BLK)),
                  cprev_spec, h0_spec,
                  pl.BlockSpec((1, 8, XBC_DIM), lambda b, c: (layer, 0, 0)),
                  lvec(XBC_DIM), lvec(LANES), lvec(LANES), lvec(SSM_DIM), lvec(SSM_DIM)],
        out_specs=[pl.BlockSpec((1, rows, SSM_DIM), lambda b, c: (b, c, 0)),
                   pl.BlockSpec((1, hp, SSM_STATE), lambda b, c: (b, 0, 0)),
                   pl.BlockSpec((1, CONV_W - 1, XBC_DIM), lambda b, c: (b, 0, 0))],
        out_shape=[jax.ShapeDtypeStruct((nb, t, SSM_DIM), out_dtype),
                   jax.ShapeDtypeStruct((nb, hp, SSM_STATE), F32),
                   jax.ShapeDtypeStruct((nb, CONV_W - 1, XBC_DIM), F32)],
        scratch_shapes=[pltpu.VMEM((8 + CHUNK, XBC_DIM), F32),
                        pltpu.VMEM((SSM_GROUPS, SSM_STATE, hp // SSM_GROUPS), F32)],
        compiler_params=_cp(2),
    )(z3, xbc3, att3, conv_prev, h0, conv_w, conv_b, dt_bias, a_log, d_skip, norm_g)


def _gelu_tanh(x):
    return 0.5 * x * (1.0 + jnp.tanh(math.sqrt(2.0 / math.pi) * (x + 0.044715 * (x * x * x))))


def _gmlp_body(gm_ref, lg_ref, lb_ref, ws_ref, bs_ref, o_ref, v_ref):
    rows = gm_ref.shape[1]
    g = _gelu_tanh(gm_ref[0])
    u = g[:, :GM_DIM]
    v = g[:, GM_DIM:]
    mu = jnp.mean(v, axis=-1, keepdims=True)
    var = jnp.mean(jnp.square(v - mu), axis=-1, keepdims=True)
    v = (v - mu) * lax.rsqrt(var + LN_EPS) * lg_ref[0] + lb_ref[0]
    v_ref[0] = v
    vb = _pad_rows(v, CHUNK).astype(BF16)
    causal = (lax.broadcasted_iota(I32, (CHUNK, CHUNK), 1) <= lax.broadcasted_iota(I32, (CHUNK, CHUNK), 0))
    gd = GM_DIM // GM_GROUPS
    bs = bs_ref[0]
    for grp in range(GM_GROUPS):
        sl = slice(grp * gd, (grp + 1) * gd)
        wm = jnp.where(causal, ws_ref[0, grp], 0.0).astype(BF16)
        mix = _dot(wm, vb[:, sl]) + bs[:, grp:grp + 1]
        o_ref[0, :, sl] = (u[:, sl] * mix[:rows]).astype(o_ref.dtype)


def _gmlp_call(gm3, layer, params, out_dtype):
    nb, t, _ = gm3.shape
    rows = min(t, CHUNK)
    ln_g, ln_b, ws, bs_t = params
    lvec = lambda w: pl.BlockSpec((1, 1, w), lambda b, c: (layer, 0, 0))
    return pl.pallas_call(
        _gmlp_body,
        grid=(nb, t // rows),
        in_specs=[pl.BlockSpec((1, rows, 2 * GM_DIM), lambda b, c: (b, c, 0)),
                  lvec(GM_DIM), lvec(GM_DIM),
                  pl.BlockSpec((1, GM_GROUPS, CHUNK, CHUNK), lambda b, c: (layer, 0, 0, 0)),
                  pl.BlockSpec((1, CHUNK, LANES), lambda b, c: (layer, 0, 0))],
        out_specs=[pl.BlockSpec((1, rows, GM_DIM), lambda b, c: (b, c, 0)),
                   pl.BlockSpec((1, rows, GM_DIM), lambda b, c: (b, c, 0))],
        out_shape=[jax.ShapeDtypeStruct((nb, t, GM_DIM), out_dtype),
                   jax.ShapeDtypeStruct((nb, t, GM_DIM), F32)],
        compiler_params=_cp(2),
    )(gm3, ln_g, ln_b, ws, bs_t)


def _merge_body(h_ref, oa_ref, ob_ref, oc_ref, wga_ref, wgb_ref, wgc_ref, wba_ref, wbb_ref, wbc_ref, m_ref):
    h = h_ref[...]
    acc = None
    for o_ref, wg_ref, wb_ref in ((oa_ref, wga_ref, wba_ref), (ob_ref, wgb_ref, wbb_ref), (oc_ref, wgc_ref, wbc_ref)):
        term = jax.nn.sigmoid(_dot(h, wg_ref[0])) * _dot(o_ref[...], wb_ref[0, 0])
        acc = term if acc is None else acc + term
    m_ref[...] = acc.astype(m_ref.dtype)


def _merge_call(h, oa, ob, oc, w_gate, w_branch, layer, tm, tn):
    n, d = h.shape
    nj = d // tn
    row = lambda w: pl.BlockSpec((tm, w), lambda i, j: (i, 0))
    wg = lambda br: pl.BlockSpec((1, d, tn), lambda i, j: (layer, 0, br * nj + j))
    wb = lambda br: pl.BlockSpec((1, 1, BRANCH_DIM, tn), lambda i, j: (layer, br, 0, j))
    return pl.pallas_call(
        _merge_body,
        grid=(n // tm, nj),
        in_specs=[row(d), row(BRANCH_DIM), row(BRANCH_DIM), row(BRANCH_DIM),
                  wg(0), wg(1), wg(2), wb(0), wb(1), wb(2)],
        out_specs=pl.BlockSpec((tm, tn), lambda i, j: (i, j)),
        out_shape=jax.ShapeDtypeStruct((n, d), BF16),
        compiler_params=_cp(2),
    )(h, oa, ob, oc, w_gate, w_gate, w_gate, w_branch, w_branch, w_branch)


def _ln_finish(acc_ref, nj, tn, lg_ref, lb_ref, sc_ref, sh_ref, xo_ref, ho_ref, po_ref):
    tm = acc_ref.shape[1]
    d = nj * tn
    s = jnp.zeros((tm, 1), F32)
    for jj in range(nj):
        s = s + jnp.sum(acc_ref[jj], axis=1, keepdims=True)
    mu = s / d
    v = jnp.zeros((tm, 1), F32)
    for jj in range(nj):
        dlt = acc_ref[jj] - mu
        v = v + jnp.sum(dlt * dlt, axis=1, keepdims=True)
    rstd = lax.rsqrt(v / d + LN_EPS)
    for jj in range(nj):
        sl = slice(jj * tn, (jj + 1) * tn)
        y = (acc_ref[jj] - mu) * rstd * lg_ref[0, :, sl] + lb_ref[0, :, sl]
        xo_ref[:, sl] = y
        hmod = y * (1.0 + sc_ref[0, :, sl]) + sh_ref[0, :, sl]
        ho_ref[:, sl] = hmod.astype(BF16)
        if po_ref is not None:
            _pack_store(po_ref, hmod, jj, tn, d)


def _pack_store(po_ref, hmod, jj, tn, d):
    half = d // 2
    bits = lax.bitcast_convert_type(hmod.astype(BF16).astype(F32), I32)
    lo_cols = jj * tn < half
    col = (jj * tn) % half
    sl = slice(col, col + tn)
    if lo_cols:
        po_ref[:, sl] = lax.shift_right_logical(bits, jnp.int32(16))
    else:
        po_ref[:, sl] = po_ref[:, sl] | (bits & jnp.int32(-65536))


def _projln_body(a_ref, w_ref, x_ref, g_ref, lg_ref, lb_ref, sc_ref, sh_ref, xo_ref, ho_ref, po_ref, acc_ref,
                 *, nj, tn, alpha):
    j = pl.program_id(1)
    acc_ref[j] = _dot(a_ref[...], w_ref[0])

    @pl.when(j == nj - 1)
    def _():
        for jj in range(nj):
            sl = slice(jj * tn, (jj + 1) * tn)
            acc_ref[jj] = alpha * x_ref[:, sl] + g_ref[0, :, sl] * acc_ref[jj]
        _ln_finish(acc_ref, nj, tn, lg_ref, lb_ref, sc_ref, sh_ref, xo_ref, ho_ref, po_ref)


def _projln_call(grp, a, w_out, x, gate, ln_g, ln_b, sc, sh, layer, alpha, tn):
    n, d = x.shape
    tm = grp.tm
    nj = d // tn
    row = lambda w: pl.BlockSpec((tm, w), lambda i, j: (i, 0))
    lvec = pl.BlockSpec((1, 1, d), lambda i, j: (layer, 0, 0))
    return pl.pallas_call(
        functools.partial(_projln_body, nj=nj, tn=tn, alpha=alpha),
        grid=(n // tm, nj),
        in_specs=[row(d), pl.BlockSpec((1, d, tn), lambda i, j: (layer, 0, j)), row(d),
                  grp.mod_spec(), lvec, lvec, grp.mod_spec(), grp.mod_spec()],
        out_specs=[row(d), row(d), row(d // 2)],
        out_shape=[jax.ShapeDtypeStruct((n, d), F32), jax.ShapeDtypeStruct((n, d), BF16),
                   jax.ShapeDtypeStruct((n, d // 2), I32)],
        scratch_shapes=[pltpu.VMEM((nj, tm, tn), F32)],
        compiler_params=_cp(2),
    )(a, w_out, x, gate, ln_g, ln_b, sc, sh)


def _combineln_body(y_ref, tw_ref, x_ref, g_ref, lg_ref, lb_ref, sc_ref, sh_ref, xo_ref, ho_ref, acc_ref,
                    *, nj, tn, alpha):
    tw = tw_ref[...]
    for jj in range(nj):
        sl = slice(jj * tn, (jj + 1) * tn)
        moe = y_ref[0, :, sl] * tw[:, 0:1]
        for k in range(1, TOP_K):
            moe = moe + y_ref[k, :, sl] * tw[:, k:k + 1]
        acc_ref[jj] = alpha * x_ref[:, sl] + g_ref[0, :, sl] * moe
    _ln_finish(acc_ref, nj, tn, lg_ref, lb_ref, sc_ref, sh_ref, xo_ref, ho_ref, None)


def _combineln_call(grp, y4, row0, tw, x, gate, ln_g, ln_b, sc, sh, layer, alpha, tm):
    n, d = x.shape
    tn = 512
    nj = d // tn
    rb = row0 // tm
    g2 = _Group(grp.n, tm, grp.rows_per_seq, grp.per_row)
    row = lambda w: pl.BlockSpec((tm, w), lambda i: (i, 0))
    lvec = pl.BlockSpec((1, 1, d), lambda i: (layer, 0, 0))
    return pl.pallas_call(
        functools.partial(_combineln_body, nj=nj, tn=tn, alpha=alpha),
        grid=(n // tm,),
        in_specs=[pl.BlockSpec((TOP_K, tm, d), lambda i: (0, rb + i, 0)),
                  pl.BlockSpec((tm, LANES), lambda i: (rb + i, 0)), row(d),
                  g2.mod_spec(), lvec, lvec, g2.mod_spec(), g2.mod_spec()],
        out_specs=[row(d), row(d)],
        out_shape=[jax.ShapeDtypeStruct((n, d), F32), jax.ShapeDtypeStruct((n, d), BF16)],
        scratch_shapes=[pltpu.VMEM((nj, tm, tn), F32)],
        compiler_params=_cp(1),
    )(y4, tw, x, gate, ln_g, ln_b, sc, sh)


def _router_body(h_ref, w_ref, b_ref, ti_ref, tw_ref):
    lg = _dot(h_ref[...], w_ref[0]) + b_ref[0]
    tm = lg.shape[0]
    lane = lax.broadcasted_iota(I32, (tm, LANES), 1)
    lg = jnp.where(lane < N_EXPERTS, lg, -jnp.inf)
    vals, idxs = [], []
    for _ in range(TOP_K):
        m = jnp.max(lg, axis=1, keepdims=True)
        idx = jnp.min(jnp.where(lg == m, lane, LANES), axis=1, keepdims=True)
        vals.append(m)
        idxs.append(idx)
        lg = jnp.where(lane == idx, -jnp.inf, lg)
    es = [jnp.exp(v - vals[0]) for v in vals]
    tot = es[0] + es[1] + es[2] + es[3]
    ti = jnp.zeros((tm, LANES), I32)
    tw = jnp.zeros((tm, LANES), F32)
    for k in range(TOP_K):
        ti = jnp.where(lane == k, idxs[k], ti)
        tw = jnp.where(lane == k, es[k] / tot, tw)
    ti_ref[...] = ti
    tw_ref[...] = tw


def _router_call(h, w_router, b_router, layer, tm):
    n, d = h.shape
    return pl.pallas_call(
        _router_body,
        grid=(n // tm,),
        in_specs=[pl.BlockSpec((tm, d), lambda i: (i, 0)),
                  pl.BlockSpec((1, d, LANES), lambda i: (layer, 0, 0)),
                  pl.BlockSpec((1, 1, LANES), lambda i: (layer, 0, 0))],
        out_specs=[pl.BlockSpec((tm, LANES), lambda i: (i, 0)), pl.BlockSpec((tm, LANES), lambda i: (i, 0))],
        out_shape=[jax.ShapeDtypeStruct((n, LANES), I32), jax.ShapeDtypeStruct((n, LANES), F32)],
        compiler_params=_cp(1),
    )(h, w_router, b_router)


def _rank_body(ti_ref, rank_ref, cnt_ref, carry_ref):
    i = pl.program_id(0)

    @pl.when(i == 0)
    def _():
        carry_ref[...] = jnp.zeros_like(carry_ref)

    ti = ti_ref[...]
    tm = ti.shape[0]
    lane = lax.broadcasted_iota(I32, (tm, LANES), 1)
    onehot = jnp.zeros((tm, LANES), F32)
    for k in range(TOP_K):
        onehot = onehot + (lane == ti[:, k:k + 1]).astype(F32)
    strict = (lax.broadcasted_iota(I32, (tm, tm), 1) < lax.broadcasted_iota(I32, (tm, tm), 0)).astype(BF16)
    before = _dot(strict, onehot.astype(BF16)) + carry_ref[0:1, :]
    rank = jnp.zeros((tm, LANES), F32)
    for k in range(TOP_K):
        rk = jnp.sum(jnp.where(lane == ti[:, k:k + 1], before, 0.0), axis=1, keepdims=True)
        rank = jnp.where(lane == k, rk, rank)
    rank_ref[...] = rank.astype(I32)
    total = carry_ref[0:1, :] + jnp.sum(onehot, axis=0, keepdims=True)
    carry_ref[...] = jnp.broadcast_to(total, carry_ref.shape)
    cnt_ref[...] = jnp.broadcast_to(total, cnt_ref.shape).astype(I32)


def _rank_call(ti, tm):
    n = ti.shape[0]
    return pl.pallas_call(
        _rank_body,
        grid=(n // tm,),
        in_specs=[pl.BlockSpec((tm, LANES), lambda i: (i, 0))],
        out_specs=[pl.BlockSpec((tm, LANES), lambda i: (i, 0)), pl.BlockSpec((8, LANES), lambda i: (0, 0))],
        out_shape=[jax.ShapeDtypeStruct((n, LANES), I32), jax.ShapeDtypeStruct((8, LANES), I32)],
        scratch_shapes=[pltpu.VMEM((8, LANES), F32)],
        compiler_params=_cp(1),
    )(ti)


def _rowcopy_body(src_idx_ref, dst_idx_ref, src_ref, *rest, per_step):
    dst_ref, sem = rest[-2:]

    def row_copy(s, d):
        return pltpu.make_async_copy(src_ref.at[pl.ds(s, 1)], dst_ref.at[pl.ds(d, 1)], sem)

    def issue(i, carry):
        row_copy(src_idx_ref[i], dst_idx_ref[i]).start()
        return carry

    lax.fori_loop(0, per_step, issue, 0)

    def wait(i, carry):
        row_copy(0, 0).wait()
        return carry

    lax.fori_loop(0, per_step, wait, 0)


def _rowcopy_call(src_idx, dst_idx, src, dst_rows, per_step, zero_fill):
    n_copies = src_idx.shape[0]
    smem = pl.BlockSpec((per_step,), lambda i: (i,), memory_space=pltpu.SMEM)
    any_spec = pl.BlockSpec(memory_space=pl.ANY)
    shape = (dst_rows, src.shape[1])
    extra = (jnp.zeros(shape, src.dtype),) if zero_fill else ()
    return pl.pallas_call(
        functools.partial(_rowcopy_body, per_step=per_step),
        grid=(n_copies // per_step,),
        in_specs=[smem, smem, any_spec] + [any_spec] * len(extra),
        out_specs=any_spec,
        out_shape=jax.ShapeDtypeStruct(shape, src.dtype),
        scratch_shapes=[pltpu.SemaphoreType.DMA(())],
        input_output_aliases={3: 0} if zero_fill else {},
        compiler_params=pltpu.CompilerParams(dimension_semantics=("arbitrary",), has_side_effects=True),
    )(src_idx, dst_idx, src, *extra)


def _unpack(xp):
    lo = lax.bitcast_convert_type(lax.shift_left(xp, jnp.int32(16)), F32).astype(BF16)
    hi = lax.bitcast_convert_type(xp & jnp.int32(-65536), F32).astype(BF16)
    return lo, hi


def _expert1_body(blk_ref, e_ref, f_ref, wf_ref, first_ref, n_ref, xs_ref, wg_ref, wl_ref, bg_ref, bl_ref, act_ref,
                  wgb_ref, wlb_ref):
    s = pl.program_id(0)

    @pl.when(s < n_ref[0])
    def _():
        @pl.when(first_ref[s] == 1)
        def _():
            wgb_ref[...] = wg_ref[0, 0].astype(BF16)
            wlb_ref[...] = wl_ref[0, 0].astype(BF16)

        lo, hi = _unpack(xs_ref[...])
        half = lo.shape[1]
        glu = _dot(lo, wgb_ref[0:half, :]) + _dot(hi, wgb_ref[half:, :]) + bg_ref[0, 0]
        lin = _dot(lo, wlb_ref[0:half, :]) + _dot(hi, wlb_ref[half:, :]) + bl_ref[0, 0]
        glu = jnp.minimum(glu, SWIGLU_LIMIT)
        lin = jnp.clip(lin, -SWIGLU_LIMIT, SWIGLU_LIMIT)
        act_ref[...] = (glu * jax.nn.sigmoid(SWIGLU_ALPHA * glu) * (lin + 1.0)).astype(BF16)

    @pl.when(s >= n_ref[0])
    def _():
        act_ref[...] = jnp.zeros_like(act_ref)


def _expert1_call(plan, xs, w1, b1, layer):
    cap, half = xs.shape
    d = 2 * half
    n_steps_max = plan["step_blk"].shape[0]
    nf = EXPERT_DIM // MOE_TF
    grid_spec = pltpu.PrefetchScalarGridSpec(
        num_scalar_prefetch=6,
        grid=(n_steps_max,),
        in_specs=[pl.BlockSpec((MOE_TM, half), lambda s, blk, e, f, wf, fi, n: (blk[s], 0)),
                  pl.BlockSpec((1, 1, d, MOE_TF), lambda s, blk, e, f, wf, fi, n: (layer, e[s], 0, wf[s])),
                  pl.BlockSpec((1, 1, d, MOE_TF), lambda s, blk, e, f, wf, fi, n: (layer, e[s], 0, nf + wf[s])),
                  pl.BlockSpec((1, 1, 1, MOE_TF), lambda s, blk, e, f, wf, fi, n: (layer, e[s], 0, wf[s])),
                  pl.BlockSpec((1, 1, 1, MOE_TF), lambda s, blk, e, f, wf, fi, n: (layer, e[s], 0, nf + wf[s]))],
        out_specs=pl.BlockSpec((MOE_TM, MOE_TF), lambda s, blk, e, f, wf, fi, n: (blk[s], f[s])),
        scratch_shapes=[pltpu.VMEM((d, MOE_TF), BF16), pltpu.VMEM((d, MOE_TF), BF16)],
    )
    return pl.pallas_call(
        _expert1_body,
        grid_spec=grid_spec,
        out_shape=jax.ShapeDtypeStruct((cap, EXPERT_DIM), BF16),
        compiler_params=_cp(1),
    )(plan["step_blk"], plan["step_e"], plan["step_f"], plan["step_wf"], plan["step_first"], plan["n_steps"],
      xs, w1, w1, b1, b1)


def _expert2_body(e_ref, first_ref, n_ref, act_ref, w_ref, b_ref, y_ref, wb_ref):
    i = pl.program_id(0)

    @pl.when(i < n_ref[0])
    def _():
        @pl.when(first_ref[i] == 1)
        def _():
            wb_ref[...] = w_ref[0, 0].astype(BF16)

        y_ref[...] = _dot(act_ref[...], wb_ref[...]) + b_ref[0, 0]

    @pl.when(i >= n_ref[0])
    def _():
        y_ref[...] = jnp.zeros_like(y_ref)


def _expert2_call(plan, act, w2, b2, layer):
    cap, f = act.shape
    d = w2.shape[-1]
    nblk = cap // MOE_TM
    grid_spec = pltpu.PrefetchScalarGridSpec(
        num_scalar_prefetch=3,
        grid=(nblk,),
        in_specs=[pl.BlockSpec((MOE_TM, f), lambda i, e, fi, n: (i, 0)),
                  pl.BlockSpec((1, 1, f, d), lambda i, e, fi, n: (layer, e[i], 0, 0)),
                  pl.BlockSpec((1, 1, 1, d), lambda i, e, fi, n: (layer, e[i], 0, 0))],
        out_specs=pl.BlockSpec((MOE_TM, d), lambda i, e, fi, n: (i, 0)),
        scratch_shapes=[pltpu.VMEM((f, d), BF16)],
    )
    return pl.pallas_call(
        _expert2_body,
        grid_spec=grid_spec,
        out_shape=jax.ShapeDtypeStruct((cap, d), F32),
        compiler_params=_cp(1),
    )(plan["blk_e"], plan["blk_first"], plan["n_blk"], act, w2, b2)


def _moe_plan(counts, top_i, rank, nblk_max):
    nb = (counts + MOE_TM - 1) // MOE_TM
    pad_end = jnp.cumsum(nb * MOE_TM)
    offsets = pad_end - nb * MOE_TM
    cumblk = jnp.cumsum(nb)
    blk_start = cumblk - nb
    n_blk = cumblk[-1]
    dest = jnp.take(offsets, top_i) + rank
    last = N_EXPERTS - 1
    bi = jnp.minimum(jnp.arange(nblk_max, dtype=I32), n_blk - 1)
    blk_e = jnp.minimum(jnp.searchsorted(cumblk, bi, side="right"), last).astype(I32)
    blk_first = (bi == jnp.take(blk_start, blk_e)).astype(I32)
    nf = EXPERT_DIM // MOE_TF
    step_end = jnp.cumsum(nf * nb)
    n_steps = step_end[-1]
    s_all = jnp.arange(nf * nblk_max, dtype=I32)
    si = jnp.minimum(s_all, n_steps - 1)
    se = jnp.minimum(jnp.searchsorted(step_end, si, side="right"), last).astype(I32)
    nbe = jnp.maximum(jnp.take(nb, se), 1)
    local = si - (jnp.take(step_end, se) - nf * jnp.take(nb, se))
    spare = s_all - n_steps
    swf = (local // nbe).astype(I32)
    sf = jnp.where(spare >= 0, spare % nf, swf).astype(I32)
    sblk = jnp.where(spare >= 0, n_blk + spare // nf, jnp.take(blk_start, se) + local % nbe).astype(I32)
    sfirst = (local % nbe == 0).astype(I32)
    return dict(dest=dest.astype(I32), blk_e=blk_e, blk_first=blk_first, n_blk=n_blk.reshape(1).astype(I32),
                step_blk=sblk, step_e=se, step_f=sf, step_wf=swf, step_first=sfirst,
                n_steps=n_steps.reshape(1).astype(I32))


def _moe(hp_all, ti_all, w1, b1, w2, b2, layer):
    n, half = hp_all.shape
    d = 2 * half
    rank_all, counts8 = _rank_call(ti_all, 64)
    counts = counts8[0, :N_EXPERTS]
    n_assign = n * TOP_K
    nblk_max = -(-(n_assign + N_EXPERTS * (MOE_TM - 1)) // MOE_TM)
    cap = nblk_max * MOE_TM
    plan = _moe_plan(counts, ti_all[:, :TOP_K], rank_all[:, :TOP_K], nblk_max)
    dest = plan["dest"].reshape(-1)
    tok = jnp.repeat(jnp.arange(n, dtype=I32), TOP_K)
    slot_row = (jnp.arange(TOP_K, dtype=I32)[None, :] * n + jnp.arange(n, dtype=I32)[:, None]).reshape(-1)
    per_step = 256
    xs = _rowcopy_call(tok, dest, hp_all, cap, per_step, True)
    act = _expert1_call(plan, xs, w1, b1, layer)
    ys = _expert2_call(plan, act, w2, b2, layer)
    y4 = _rowcopy_call(dest, slot_row, ys, TOP_K * n, per_step, False)
    return y4.reshape(TOP_K, n, d)


def _mixers(grp, h, att, z, xbc, gm, nb, t, layer, prm, tables, tab_blocks, past, ssm_state, out_dtype):
    k_f, ki_f, kb, vb, kib = _kvpost_call(att, tables, tab_blocks, prm["idx_kn_g"], prm["idx_kn_b"], layer,
                                          min(grp.n, 512))
    att3 = att.reshape(nb, t, ATT_COLS)
    if past is None:
        nk = t
        q_pos0 = 0
        kb3, vb3, kib3 = (a.reshape(nb, t, -1) for a in (kb, vb, kib))
        tq = CHUNK
    else:
        page_table, cache_k, cache_v, cache_idx_k = past
        v_f = att[:, ATT_V_BLK * KV_DIM:(ATT_V_BLK + 1) * KV_DIM]
        kb3, vb3, kib3 = _pages_call(page_table, cache_k, cache_v, cache_idx_k, k_f.reshape(nb, t, KV_DIM),
                                     v_f.reshape(nb, t, KV_DIM), ki_f.reshape(nb, t, IDX_DIM), layer)
        q_pos0 = page_table.shape[1] * PAGE_SIZE
        nk = q_pos0 + prm["t_valid"]
        tq = t
    topk = min(TOPK_MAX, nk // 4)
    o_a = _dsa_call(att3, tables, kb3, vb3, kib3, tq, q_pos0, topk, out_dtype)
    ssm_params = (prm["conv_w"], prm["conv_b"], prm["dt_bias"], prm["a_log"], prm["d_skip"], prm["ssm_norm_g"])
    conv_prev, h0, layered = ssm_state
    valid_last = prm["t_valid"] if t < CHUNK else CHUNK
    o_b, ssm_new, conv_new = _ssd_call(z.reshape(nb, t, SSM_DIM), xbc.reshape(nb, t, XBC_DIM), att3, conv_prev, h0,
                                       layer, ssm_params, valid_last, out_dtype, layered)
    gm_params = (prm["gm_ln_g"], prm["gm_ln_b"], prm["gm_ws"], prm["gm_bs_t"])
    o_c, gm_v = _gmlp_call(gm.reshape(nb, t, 2 * GM_DIM), layer, gm_params, out_dtype)
    flat = lambda a: a.reshape(nb * t, -1).astype(BF16)
    return (flat(o_a), flat(o_b), flat(o_c)), (k_f, ki_f, ssm_new, conv_new, gm_v)


def kernel(x_prompt, x_sample, cache_k, cache_v, cache_idx_k, state_ssm, state_conv, page_table, c_prompt, c_sample, ln1_g, ln1_b, ln2_g, ln2_b, w_ada, b_ada, w_in, idx_kn_g, idx_kn_b, conv_w, conv_b, dt_bias, a_log, d_skip, ssm_norm_g, gm_ln_g, gm_ln_b, gm_ws, gm_bs, w_gate, w_branch, w_out, w_router, b_router, w1, b1, w2, b2):
    bp, sp, d = x_prompt.shape
    db, ds, _ = x_sample.shape
    depth = w_in.shape[0]
    alpha = (2 * depth) ** 0.25
    past_len = page_table.shape[1] * PAGE_SIZE
    n_p, n_s = bp * sp, db * SAMPLE_ROWS
    grp_p = _Group(n_p, 256, sp, False)
    grp_s = _Group(n_s, n_s, SAMPLE_ROWS, True)

    o = np.cumsum([0, ATT_DIM, KV_DIM, KV_DIM, IDX_HEADS * IDX_DIM, IDX_DIM, IDX_HEADS, SSM_DIM, XBC_DIM, SSM_HEADS,
                   2 * GM_DIM])
    seg = lambda i, j=None: w_in[:, :, o[i]:o[(i if j is None else j) + 1]].astype(BF16)
    pad_cols = LANES - SSM_HEADS - IDX_HEADS
    w_proj = jnp.concatenate([seg(6), seg(9), seg(7), seg(0, 4), seg(8), seg(5),
                              jnp.zeros((depth, d, pad_cols), BF16)], axis=-1)
    w_gate_b = w_gate.astype(BF16)
    w_branch_b = w_branch.astype(BF16)
    w_out_b = w_out.astype(BF16)
    w_router_b = jnp.pad(w_router, ((0, 0), (0, 0), (0, LANES - N_EXPERTS))).astype(BF16)
    lane_pad = lambda a: jnp.pad(a, ((0, 0), (0, LANES - a.shape[-1])))[:, None, :]
    vec = lambda a: a[:, None, :]
    prm = dict(
        idx_kn_g=vec(idx_kn_g), idx_kn_b=vec(idx_kn_b),
        conv_w=jnp.pad(conv_w, ((0, 0), (0, 8 - CONV_W), (0, 0))), conv_b=vec(conv_b),
        dt_bias=lane_pad(dt_bias), a_log=lane_pad(a_log),
        d_skip=vec(jnp.repeat(d_skip, SSM_HEAD_DIM, axis=-1)), ssm_norm_g=vec(ssm_norm_g),
        gm_ln_g=vec(gm_ln_g), gm_ln_b=vec(gm_ln_b), gm_ws=gm_ws,
        gm_bs_t=jnp.pad(jnp.swapaxes(gm_bs, 1, 2), ((0, 0), (0, 0), (0, LANES - GM_GROUPS))),
    )
    b_router_p = lane_pad(b_router)
    b1_4 = b1[:, :, None, :]
    b2_4 = b2[:, :, None, :]
    ln = dict(g1=vec(ln1_g), b1=vec(ln1_b), g2=vec(ln2_g), b2=vec(ln2_b))

    tab_p = _rope_tables(jnp.arange(sp, dtype=I32))
    pos_s = past_len + jnp.arange(SAMPLE_ROWS, dtype=I32)
    tab_s1 = _rope_tables(pos_s)
    tab_s = tuple(jnp.tile(a, (db, 1)) for a in tab_s1)

    n_c = bp + db
    c_rows = jnp.concatenate([c_prompt, c_sample, jnp.zeros((-n_c % 8, d), F32)], axis=0)
    mod = _mod_call(c_rows, w_ada, b_ada).reshape(depth, c_rows.shape[0], 6, d)

    def mods(layer, which, grp, lo, hi):
        return grp.mod_array(mod[layer, lo:hi, which])

    zero_mod_p = grp_p.mod_array(jnp.zeros((bp, d), F32))
    zero_mod_s = grp_s.mod_array(jnp.zeros((db, d), F32))

    xs_pad = jnp.pad(x_sample, ((0, 0), (0, SAMPLE_ROWS - ds), (0, 0))).reshape(n_s, d)
    xp = x_prompt.reshape(n_p, d)
    xs_ = xs_pad
    hp = _modulate_call(grp_p, xp, mods(0, 1, grp_p, 0, bp), mods(0, 0, grp_p, 0, bp))
    hs = _modulate_call(grp_s, xs_, mods(0, 1, grp_s, bp, n_c), mods(0, 0, grp_s, bp, n_c))

    conv0 = jnp.zeros((bp, CONV_W - 1, XBC_DIM), F32)
    ssm0 = jnp.zeros((bp, SSM_HEADS * SSM_HEAD_DIM, SSM_STATE), F32)
    state_ssm4 = state_ssm.reshape(db, depth, SSM_HEADS * SSM_HEAD_DIM, SSM_STATE)

    outs_p = [[] for _ in range(5)]
    outs_s = [[] for _ in range(6)]
    for layer in range(depth):
        groups = (
            (grp_p, hp, xp, bp, sp, tab_p, sp // min(n_p, 512), None, (conv0, ssm0, False), BF16, min(n_p, 1024),
             0, bp),
            (grp_s, hs, xs_, db, SAMPLE_ROWS, tab_s, 1, (page_table, cache_k, cache_v, cache_idx_k),
             (state_conv, state_ssm4, True), F32, n_s, bp, n_c),
        )
        x1s, h2s, packs, tis, tws = [], [], [], [], []
        for grp, h, x, nb, t, tabs, tab_blocks, past, ssm_state, odt, tm_mm, lo, hi in groups:
            z = _mm_call(h, w_proj, layer, COL_Z, SSM_DIM, tm_mm, 512)
            gm = _mm_call(h, w_proj, layer, COL_GM, 2 * GM_DIM, tm_mm, 512)
            xbc = _mm_call(h, w_proj, layer, COL_XBC, XBC_DIM, tm_mm, 512)
            att = _mm_call(h, w_proj, layer, COL_ATT, ATT_COLS, tm_mm, 256)
            prm_g = dict(prm, t_valid=(t if past is None else ds))
            branches, new = _mixers(grp, h, att, z, xbc, gm, nb, t, layer, prm_g, tabs, tab_blocks, past, ssm_state, odt)
            k_f, ki_f, ssm_new, conv_new, gm_v = new
            v_f = att[:, ATT_V_BLK * KV_DIM:(ATT_V_BLK + 1) * KV_DIM]
            if past is None:
                outs_p[0].append(k_f.reshape(nb, t, N_KV_HEADS, HEAD_DIM))
                outs_p[1].append(v_f.reshape(nb, t, N_KV_HEADS, HEAD_DIM))
                outs_p[2].append(ki_f.reshape(nb, t, IDX_DIM))
                outs_p[3].append(ssm_new.reshape(nb, SSM_HEADS, SSM_HEAD_DIM, SSM_STATE))
                outs_p[4].append(conv_new)
            else:
                outs_s[0].append(k_f.reshape(nb, t, N_KV_HEADS, HEAD_DIM)[:, :ds])
                outs_s[1].append(v_f.reshape(nb, t, N_KV_HEADS, HEAD_DIM)[:, :ds])
                outs_s[2].append(ki_f.reshape(nb, t, IDX_DIM)[:, :ds])
                outs_s[3].append(ssm_new.reshape(nb, SSM_HEADS, SSM_HEAD_DIM, SSM_STATE))
                outs_s[4].append(conv_new)
                outs_s[5].append(gm_v[:, :ds])
            merged = _merge_call(h, *branches, w_gate_b, w_branch_b, layer, min(grp.n, 512), 256)
            x1, h2, pack = _projln_call(grp, merged, w_out_b, x, mods(layer, 2, grp, lo, hi), ln["g1"], ln["b1"],
                                        mods(layer, 4, grp, lo, hi), mods(layer, 3, grp, lo, hi), layer, alpha, 512)
            ti, tw = _router_call(h2, w_router_b, b_router_p, layer, min(grp.n, 512))
            x1s.append(x1)
            packs.append(pack)
            tis.append(ti)
            tws.append(tw)
        hp_all = jnp.concatenate(packs, axis=0)
        ti_all = jnp.concatenate(tis, axis=0)
        tw_all = jnp.concatenate(tws, axis=0)
        y4 = _moe(hp_all, ti_all, w1, b1_4, w2, b2_4, layer)
        nxt = layer + 1
        new_x, new_h = [], []
        for (grp, lo, hi, zero_mod, row0, tm), x1 in zip(
                ((grp_p, 0, bp, zero_mod_p, 0, 128), (grp_s, bp, n_c, zero_mod_s, n_p, n_s)), x1s):
            sc = mods(nxt, 1, grp, lo, hi) if nxt < depth else zero_mod
            sh = mods(nxt, 0, grp, lo, hi) if nxt < depth else zero_mod
            xo, ho = _combineln_call(grp, y4, row0, tw_all, x1, mods(layer, 5, grp, lo, hi), ln["g2"], ln["b2"],
                                     sc, sh, layer, alpha, tm)
            new_x.append(xo)
            new_h.append(ho)
        (xp, xs_), (hp, hs) = new_x, new_h

    stack = lambda lst: jnp.stack(lst, axis=1)
    y_prompt = xp.reshape(bp, sp, d)
    y_sample = xs_.reshape(db, SAMPLE_ROWS, d)[:, :ds]
    return (y_prompt, y_sample, *(stack(a) for a in outs_p), *(stack(a) for a in outs_s))
```

```python
import functools
import math

import jax
import jax.numpy as jnp
import numpy as np
from jax import lax
from jax.experimental import pallas as pl
from jax.experimental.pallas import tpu as pltpu

F32 = jnp.float32
BF16 = jnp.bfloat16
I32 = jnp.int32

D_MODEL = 4096
PAGE_SIZE = 128
N_HEADS = 16
HEAD_DIM = 128
N_KV_HEADS = 4
Q_PER_KV = N_HEADS // N_KV_HEADS
ATT_DIM = N_HEADS * HEAD_DIM
KV_DIM = N_KV_HEADS * HEAD_DIM
ROT_DIM = HEAD_DIM // 4
IDX_HEADS = 8
IDX_DIM = 128
TOPK_MAX = 256
ROPE_THETA = 500000.0
ATT_SCALE = HEAD_DIM ** -0.5
IDX_SCALE = (IDX_DIM * IDX_HEADS) ** -0.5
SSM_DIM = 2048
SSM_HEAD_DIM = 64
SSM_HEADS = SSM_DIM // SSM_HEAD_DIM
SSM_GROUPS = 4
SSM_STATE = 128
CONV_W = 4
XBC_DIM = SSM_DIM + 2 * SSM_GROUPS * SSM_STATE
GM_DIM = 2048
GM_GROUPS = 16
N_BRANCH = 3
BRANCH_DIM = 2048
N_EXPERTS = 32
TOP_K = 4
EXPERT_DIM = 1024
SWIGLU_ALPHA = 1.702
SWIGLU_LIMIT = 7.0
LN_EPS = 1e-5
RMS_EPS = 1e-5

CHUNK = 128
SAMPLE_ROWS = 8
LANES = 128
COL_Z = 0
COL_GM = COL_Z + SSM_DIM
COL_XBC = COL_GM + 2 * GM_DIM
COL_ATT = COL_XBC + XBC_DIM
ATT_COLS = ATT_DIM + 2 * KV_DIM + IDX_HEADS * IDX_DIM + IDX_DIM + LANES
PROJ_COLS = COL_ATT + ATT_COLS
ATT_K_BLK = ATT_DIM // KV_DIM
ATT_V_BLK = ATT_K_BLK + 1
ATT_QI_BLK = (ATT_DIM + 2 * KV_DIM) // (IDX_HEADS * IDX_DIM)
ATT_KI_BLK = (ATT_DIM + 2 * KV_DIM + IDX_HEADS * IDX_DIM) // LANES
ATT_DTWI_BLK = ATT_KI_BLK + 1
WI_LANE = SSM_HEADS
MOE_TM = 256
MOE_TF = 512
VMEM_LIMIT = 56 * 1024 * 1024


def _cp(n_axes, vmem=VMEM_LIMIT):
    return pltpu.CompilerParams(dimension_semantics=("arbitrary",) * n_axes, vmem_limit_bytes=vmem)


def _dot(a, b):
    return jnp.dot(a, b, preferred_element_type=F32)


def _dot_nt(a, b):
    return lax.dot_general(a, b, (((1,), (1,)), ((), ())), preferred_element_type=F32)


def _split3(x):
    hi = x.astype(BF16)
    r1 = x - hi.astype(F32)
    mid = r1.astype(BF16)
    lo = (r1 - mid.astype(F32)).astype(BF16)
    return hi, mid, lo


def _dot3_left(m_bf16, x):
    hi, mid, lo = _split3(x)
    return _dot(m_bf16, hi) + _dot(m_bf16, mid) + _dot(m_bf16, lo)


def _dot3_right(x, m_bf16):
    hi, mid, lo = _split3(x)
    return _dot(hi, m_bf16) + _dot(mid, m_bf16) + _dot(lo, m_bf16)


def _silu(x):
    return x * jax.nn.sigmoid(x)


def _rope(x, c, s1, s2):
    half = ROT_DIM // 2
    return x * c + pltpu.roll(x, LANES - half, 1) * s1 + pltpu.roll(x, half, 1) * s2


def _rope_tables(pos):
    half = ROT_DIM // 2
    inv = ROPE_THETA ** (-jnp.arange(half, dtype=F32) / half)
    ang = pos.astype(F32)[:, None] * inv
    cos, sin = jnp.cos(ang), jnp.sin(ang)
    n = pos.shape[0]
    rest = LANES - ROT_DIM
    c = jnp.concatenate([cos, cos, jnp.ones((n, rest), F32)], axis=1)
    s1 = jnp.concatenate([-sin, jnp.zeros((n, LANES - half), F32)], axis=1)
    s2 = jnp.concatenate([jnp.zeros((n, half), F32), sin, jnp.zeros((n, rest), F32)], axis=1)
    return c, s1, s2


def _mod_body(c_ref, w_ref, b_ref, o_ref):
    s = _silu(c_ref[...]).astype(BF16)
    o_ref[0] = _dot(s, w_ref[0].astype(BF16)) + b_ref[0]


def _mod_call(c_rows, w_ada, b_ada):
    n_layers, d, n = w_ada.shape
    r = c_rows.shape[0]
    tn = 512
    return pl.pallas_call(
        _mod_body,
        grid=(n_layers, n // tn),
        in_specs=[pl.BlockSpec((r, d), lambda l, j: (0, 0)),
                  pl.BlockSpec((1, d, tn), lambda l, j: (l, 0, j)),
                  pl.BlockSpec((1, 1, tn), lambda l, j: (l, 0, j))],
        out_specs=pl.BlockSpec((1, r, tn), lambda l, j: (l, 0, j)),
        out_shape=jax.ShapeDtypeStruct((n_layers, r, n), F32),
        compiler_params=_cp(2),
    )(c_rows, w_ada, b_ada.reshape(n_layers, 1, n))


class _Group:
    def __init__(self, n, tm, rows_per_seq, per_row):
        self.n, self.tm, self.rows_per_seq, self.per_row = n, tm, rows_per_seq, per_row

    def mod_array(self, m):
        if self.per_row:
            return jnp.repeat(m, self.rows_per_seq, axis=0)[None]
        return m[:, None, :]

    def mod_spec(self, width=D_MODEL):
        if self.per_row:
            return pl.BlockSpec((1, self.n, width), lambda i, *_: (0, 0, 0))
        tm, rps = self.tm, self.rows_per_seq
        return pl.BlockSpec((1, 1, width), lambda i, *_: ((i * tm) // rps, 0, 0))


def _modulate_body(x_ref, sc_ref, sh_ref, h_ref):
    h_ref[...] = (x_ref[...] * (1.0 + sc_ref[0]) + sh_ref[0]).astype(BF16)


def _modulate_call(grp, x, sc, sh):
    n, d = x.shape
    tm = grp.tm
    return pl.pallas_call(
        _modulate_body,
        grid=(n // tm,),
        in_specs=[pl.BlockSpec((tm, d), lambda i: (i, 0)), grp.mod_spec(), grp.mod_spec()],
        out_specs=pl.BlockSpec((tm, d), lambda i: (i, 0)),
        out_shape=jax.ShapeDtypeStruct((n, d), BF16),
        compiler_params=_cp(1),
    )(x, sc, sh)


def _mm_body(a_ref, w_ref, o_ref):
    o_ref[...] = _dot(a_ref[...], w_ref[0]).astype(o_ref.dtype)


def _mm_call(a, w, layer, col0, ncols, tm, tn, out_dtype=F32):
    n, k = a.shape
    cb = col0 // tn
    return pl.pallas_call(
        _mm_body,
        grid=(n // tm, ncols // tn),
        in_specs=[pl.BlockSpec((tm, k), lambda i, j: (i, 0)),
                  pl.BlockSpec((1, k, tn), lambda i, j: (layer, 0, cb + j))],
        out_specs=pl.BlockSpec((tm, tn), lambda i, j: (i, j)),
        out_shape=jax.ShapeDtypeStruct((n, ncols), out_dtype),
        compiler_params=_cp(2),
    )(a, w)


def _kvpost_body(k_ref, v_ref, ki_ref, c_ref, s1_ref, s2_ref, g_ref, b_ref,
                 ko_ref, kio_ref, kb_ref, vb_ref, kib_ref):
    c, s1, s2 = c_ref[...], s1_ref[...], s2_ref[...]
    for h in range(N_KV_HEADS):
        sl = slice(h * HEAD_DIM, (h + 1) * HEAD_DIM)
        kh = _rope(k_ref[:, sl], c, s1, s2)
        ko_ref[:, sl] = kh
        kb_ref[:, sl] = kh.astype(BF16)
    vb_ref[...] = v_ref[...].astype(BF16)
    ki = ki_ref[...]
    mu = jnp.mean(ki, axis=-1, keepdims=True)
    var = jnp.mean(jnp.square(ki - mu), axis=-1, keepdims=True)
    kin = (ki - mu) * lax.rsqrt(var + LN_EPS) * g_ref[0] + b_ref[0]
    kir = _rope(kin, c, s1, s2)
    kio_ref[...] = kir
    kib_ref[...] = kir.astype(BF16)


def _kvpost_call(att, tables, tab_blocks, g, b, layer, tm):
    n = att.shape[0]
    tab = pl.BlockSpec((tm, LANES), lambda i: (i % tab_blocks, 0))
    vec = pl.BlockSpec((1, 1, LANES), lambda i: (layer, 0, 0))
    row = lambda w: pl.BlockSpec((tm, w), lambda i: (i, 0))
    return pl.pallas_call(
        _kvpost_body,
        grid=(n // tm,),
        in_specs=[pl.BlockSpec((tm, KV_DIM), lambda i: (i, ATT_K_BLK)),
                  pl.BlockSpec((tm, KV_DIM), lambda i: (i, ATT_V_BLK)),
                  pl.BlockSpec((tm, IDX_DIM), lambda i: (i, ATT_KI_BLK)),
                  tab, tab, tab, vec, vec],
        out_specs=[row(KV_DIM), row(IDX_DIM), row(KV_DIM), row(KV_DIM), row(IDX_DIM)],
        out_shape=[jax.ShapeDtypeStruct((n, KV_DIM), F32), jax.ShapeDtypeStruct((n, IDX_DIM), F32),
                   jax.ShapeDtypeStruct((n, KV_DIM), BF16), jax.ShapeDtypeStruct((n, KV_DIM), BF16),
                   jax.ShapeDtypeStruct((n, IDX_DIM), BF16)],
        compiler_params=_cp(1),
    )(att, att, att, *tables, g, b)


def _pages_body(pt_ref, *refs, n_steps, pps):
    ck_refs, cv_refs, ci_refs = refs[0:pps], refs[pps:2 * pps], refs[2 * pps:3 * pps]
    kn_ref, vn_ref, in_ref, ko_ref, vo_ref, io_ref = refs[3 * pps:]
    p = pl.program_id(1)

    @pl.when(p < n_steps)
    def _():
        for j in range(pps):
            rows = slice(j * PAGE_SIZE, (j + 1) * PAGE_SIZE)
            for h in range(N_KV_HEADS):
                sl = slice(h * HEAD_DIM, (h + 1) * HEAD_DIM)
                ko_ref[0, rows, sl] = ck_refs[j][:, h, :].astype(BF16)
                vo_ref[0, rows, sl] = cv_refs[j][:, h, :].astype(BF16)
            io_ref[0, rows, :] = ci_refs[j][...].astype(BF16)

    @pl.when(p == n_steps)
    def _():
        r = kn_ref.shape[1]
        tail = pps * PAGE_SIZE - r
        ko_ref[0] = jnp.concatenate([kn_ref[0], jnp.zeros((tail, KV_DIM), F32)], axis=0).astype(BF16)
        vo_ref[0] = jnp.concatenate([vn_ref[0], jnp.zeros((tail, KV_DIM), F32)], axis=0).astype(BF16)
        io_ref[0] = jnp.concatenate([in_ref[0], jnp.zeros((tail, IDX_DIM), F32)], axis=0).astype(BF16)


def _pages_call(page_table, cache_k, cache_v, cache_idx_k, k_new, v_new, ki_new, layer):
    nb, n_pages = page_table.shape
    r = k_new.shape[1]
    pps = math.gcd(n_pages, 8)
    n_steps = n_pages // pps
    blk = pps * PAGE_SIZE
    nk = (n_steps + 1) * blk

    def page_map(j, extra):
        def index_map(b, p, pt):
            return (pt[b * n_pages + jnp.minimum(p, n_steps - 1) * pps + j], layer) + (0,) * extra
        return index_map

    kv_spec = lambda j: pl.BlockSpec((None, None, PAGE_SIZE, N_KV_HEADS, HEAD_DIM), page_map(j, 3))
    ix_spec = lambda j: pl.BlockSpec((None, None, PAGE_SIZE, IDX_DIM), page_map(j, 2))
    new_spec = lambda w: pl.BlockSpec((1, r, w), lambda b, p, pt: (b, 0, 0))
    out_spec = lambda w: pl.BlockSpec((1, blk, w), lambda b, p, pt: (b, p, 0))
    grid_spec = pltpu.PrefetchScalarGridSpec(
        num_scalar_prefetch=1,
        grid=(nb, n_steps + 1),
        in_specs=([kv_spec(j) for j in range(pps)] + [kv_spec(j) for j in range(pps)]
                  + [ix_spec(j) for j in range(pps)] + [new_spec(KV_DIM), new_spec(KV_DIM), new_spec(IDX_DIM)]),
        out_specs=[out_spec(KV_DIM), out_spec(KV_DIM), out_spec(IDX_DIM)],
    )
    return pl.pallas_call(
        functools.partial(_pages_body, n_steps=n_steps, pps=pps),
        grid_spec=grid_spec,
        out_shape=[jax.ShapeDtypeStruct((nb, nk, KV_DIM), BF16), jax.ShapeDtypeStruct((nb, nk, KV_DIM), BF16),
                   jax.ShapeDtypeStruct((nb, nk, IDX_DIM), BF16)],
        compiler_params=_cp(2),
    )(page_table.reshape(-1), *([cache_k] * pps), *([cache_v] * pps), *([cache_idx_k] * pps), k_new, v_new, ki_new)


def _sort_key(score):
    bits = lax.bitcast_convert_type(score, I32)
    return jnp.where(bits < 0, bits ^ jnp.int32(0x7FFFFFFF), bits)


def _kth_largest(key, k):
    kf = jnp.float32(k)

    def count_ge(c):
        return jnp.sum((key >= c).astype(F32), axis=1, keepdims=True)

    int_min = jnp.int32(-2 ** 31)
    prefix = jnp.where(count_ge(jnp.int32(0)) >= kf, jnp.int32(0), int_min)

    def body(i, prefix):
        cand = prefix | lax.shift_left(jnp.int32(1), jnp.int32(30) - i)
        return jnp.where(count_ge(cand) >= kf, cand, prefix)

    return lax.fori_loop(0, 31, body, prefix)


def _select_topk(key, kpos, k):
    thr = _kth_largest(key, k)
    gt = key > thr
    eq = key == thr
    need = jnp.float32(k) - jnp.sum(gt.astype(F32), axis=1, keepdims=True)
    nbits = int(math.ceil(math.log2(key.shape[1] + 1)))

    def body(i, bound):
        cand = bound | lax.shift_left(jnp.int32(1), jnp.int32(nbits - 1) - i)
        cnt = jnp.sum((eq & (kpos < cand)).astype(F32), axis=1, keepdims=True)
        return jnp.where(cnt <= need, cand, bound)

    n_eq = jnp.sum(eq.astype(F32), axis=1, keepdims=True)
    tied = jnp.max(n_eq - need) > 0.0
    full = jnp.full_like(thr, 2 ** nbits - 1)
    bound = lax.cond(tied, lambda: lax.fori_loop(0, nbits, body, jnp.zeros_like(thr)), lambda: full)
    return gt | (eq & (kpos < bound))


def _dsa_body(q_ref, qi_ref, dtwi_ref, c_ref, s1_ref, s2_ref, k_ref, v_ref, ki_ref, o_ref, *, q_pos0, topk):
    tq = q_ref.shape[1]
    nk = k_ref.shape[1]
    c, s1, s2 = c_ref[...], s1_ref[...], s2_ref[...]
    ki = ki_ref[0]
    dtwi = dtwi_ref[0]
    score = jnp.zeros((tq, nk), F32)
    for h in range(IDX_HEADS):
        qh = _rope(qi_ref[0, :, h * IDX_DIM:(h + 1) * IDX_DIM], c, s1, s2).astype(BF16)
        d = _dot_nt(qh, ki)
        score = score + jnp.maximum(d, 0.0) * dtwi[:, WI_LANE + h:WI_LANE + h + 1]
    score = score * IDX_SCALE
    kpos = lax.broadcasted_iota(I32, (tq, nk), 1)
    qpos = q_pos0 + pl.program_id(1) * tq + lax.broadcasted_iota(I32, (tq, nk), 0)
    visible = kpos <= qpos
    score = jnp.where(visible, score, -jnp.inf)
    sel = _select_topk(_sort_key(score), kpos, topk) & visible
    bias = jnp.where(sel, 0.0, -jnp.inf)[None]
    for kvh in range(N_KV_HEADS):
        heads = [kvh * Q_PER_KV + g for g in range(Q_PER_KV)]
        qh = jnp.concatenate(
            [_rope(q_ref[0, :, h * HEAD_DIM:(h + 1) * HEAD_DIM], c, s1, s2) for h in heads], axis=0).astype(BF16)
        sl = slice(kvh * HEAD_DIM, (kvh + 1) * HEAD_DIM)
        logits = _dot_nt(qh, k_ref[0, :, sl]) * ATT_SCALE
        logits = logits.reshape(Q_PER_KV, tq, nk) + bias
        m = jnp.max(logits, axis=-1, keepdims=True)
        p = jnp.exp(logits - m)
        denom = jnp.sum(p, axis=-1, keepdims=True)
        o = _dot(p.astype(BF16).reshape(Q_PER_KV * tq, nk), v_ref[0, :, sl])
        o = o.reshape(Q_PER_KV, tq, HEAD_DIM) / denom
        for g, h in enumerate(heads):
            o_ref[0, :, h * HEAD_DIM:(h + 1) * HEAD_DIM] = o[g].astype(o_ref.dtype)


def _dsa_call(att3, tables, kb, vb, kib, tq, q_pos0, topk, out_dtype, blk0=0, n_blk=None, nk=None):
    nb, tq_total, _ = att3.shape
    nk = kb.shape[1] if nk is None else nk
    n_blk = tq_total // tq if n_blk is None else n_blk
    tab = pl.BlockSpec((tq, LANES), lambda b, j: (blk0 + j, 0))
    return pl.pallas_call(
        functools.partial(_dsa_body, q_pos0=q_pos0 + blk0 * tq, topk=topk),
        grid=(nb, n_blk),
        in_specs=[pl.BlockSpec((1, tq, ATT_DIM), lambda b, j: (b, blk0 + j, 0)),
                  pl.BlockSpec((1, tq, IDX_HEADS * IDX_DIM), lambda b, j: (b, blk0 + j, ATT_QI_BLK)),
                  pl.BlockSpec((1, tq, LANES), lambda b, j: (b, blk0 + j, ATT_DTWI_BLK)),
                  tab, tab, tab,
                  pl.BlockSpec((1, nk, KV_DIM), lambda b, j: (b, 0, 0)),
                  pl.BlockSpec((1, nk, KV_DIM), lambda b, j: (b, 0, 0)),
                  pl.BlockSpec((1, nk, IDX_DIM), lambda b, j: (b, 0, 0))],
        out_specs=pl.BlockSpec((1, tq, ATT_DIM), lambda b, j: (b, j, 0)),
        out_shape=jax.ShapeDtypeStruct((nb, n_blk * tq, ATT_DIM), out_dtype),
        compiler_params=_cp(2),
    )(att3, att3, att3, *tables, kb, vb, kib)


def _pad_rows(x, rows):
    r = x.shape[0]
    if r == rows:
        return x
    return jnp.concatenate([x, jnp.zeros((rows - r,) + x.shape[1:], x.dtype)], axis=0)


def _ssd_body(z_ref, xbc_ref, dtwi_ref, cprev_ref, h0_ref, cw_ref, cb_ref, dtb_ref, alog_ref, dskip_ref, ng_ref,
              y_ref, hout_ref, cnew_ref, xp_ref, ht_ref, *, valid_last, n_chunks):
    ci = pl.program_id(1)
    rows = xbc_ref.shape[1]
    e_heads = SSM_HEADS // SSM_GROUPS
    gw = e_heads * SSM_HEAD_DIM
    pre = 8

    @pl.when(ci == 0)
    def _():
        xp_ref[0:pre, :] = _pad_rows(jnp.concatenate(
            [jnp.zeros((pre - (CONV_W - 1), XBC_DIM), F32), cprev_ref[0]], axis=0), pre)
        for g in range(SSM_GROUPS):
            ht_ref[g] = h0_ref[0, g * gw:(g + 1) * gw, :].T

    xp_ref[pre:pre + CHUNK, :] = _pad_rows(xbc_ref[0], CHUNK)
    conv = cb_ref[0]
    for i in range(CONV_W):
        off = pre - (CONV_W - 1) + i
        conv = conv + xp_ref[off:off + CHUNK, :] * cw_ref[0, i:i + 1, :]
    last = valid_last if n_chunks == 1 else CHUNK

    @pl.when(ci == n_chunks - 1)
    def _():
        cnew_ref[0] = xp_ref[pre + valid_last - (CONV_W - 1):pre + valid_last, :]

    if n_chunks > 1:
        tail = xp_ref[pre + last - (CONV_W - 1):pre + last, :]
        xp_ref[pre - (CONV_W - 1):pre, :] = tail

    u = _silu(conv)
    xs = u[:, :SSM_DIM]
    bm = u[:, SSM_DIM:SSM_DIM + SSM_GROUPS * SSM_STATE]
    cm = u[:, SSM_DIM + SSM_GROUPS * SSM_STATE:]

    dt_raw = _pad_rows(dtwi_ref[0], CHUNK)
    lane = lax.broadcasted_iota(I32, (CHUNK, LANES), 1)
    rowi = lax.broadcasted_iota(I32, (CHUNK, LANES), 0)
    xdt_in = dt_raw + dtb_ref[0]
    dt = jnp.maximum(xdt_in, 0.0) + jnp.log1p(jnp.exp(-jnp.abs(xdt_in)))
    row_ok = rowi < jnp.where(ci == n_chunks - 1, valid_last, CHUNK)
    dt = jnp.where((lane < SSM_HEADS) & row_ok, dt, 0.0)
    a = -jnp.exp(alog_ref[0])
    da = dt * a
    tri = (lax.broadcasted_iota(I32, (CHUNK, CHUNK), 1) <= lax.broadcasted_iota(I32, (CHUNK, CHUNK), 0))
    cum = _dot3_left(tri.astype(BF16), da)
    cum_t = cum.T
    hh = lax.broadcasted_iota(I32, (LANES, SSM_DIM), 0)
    cc = lax.broadcasted_iota(I32, (LANES, SSM_DIM), 1)
    expand = (cc // SSM_HEAD_DIM == hh).astype(BF16)
    dt_x = _dot3_right(dt, expand)
    cum_x = _dot3_right(cum, expand)
    cum_last = cum[CHUNK - 1:CHUNK, :]
    cum_last_x = cum_x[CHUNK - 1:CHUNK, :]
    xdt = xs * dt_x
    grow = jnp.exp(cum_x)
    xdt_end = (xdt * jnp.exp(cum_last_x - cum_x)).astype(BF16)
    xdt_b = xdt.astype(BF16)
    lane_lo = lax.broadcasted_iota(I32, (CHUNK, LANES), 1) < SSM_HEAD_DIM

    for g in range(SSM_GROUPS):
        st = slice(g * SSM_STATE, (g + 1) * SSM_STATE)
        ch = slice(g * gw, (g + 1) * gw)
        bm_g = bm[:, st].astype(BF16)
        cm_g = cm[:, st].astype(BF16)
        cb = _dot_nt(cm_g, bm_g)
        h_prev = ht_ref[g]
        y_off = _dot(cm_g, h_prev.astype(BF16)) * grow[:, ch]
        parts = []
        for pr in range(e_heads // 2):
            acc = None
            pair = slice(g * gw + pr * LANES, g * gw + (pr + 1) * LANES)
            xp_pair = xdt_b[:, pair]
            for half in range(2):
                h = g * e_heads + pr * 2 + half
                seg = cum[:, h:h + 1] - cum_t[h:h + 1, :]
                dec = jnp.exp(jnp.where(tri, seg, -jnp.inf))
                m = (cb * dec).astype(BF16)
                keep = lane_lo if half == 0 else jnp.logical_not(lane_lo)
                part = _dot(m, jnp.where(keep, xp_pair, jnp.zeros_like(xp_pair)))
                acc = part if acc is None else acc + part
            parts.append(acc)
        y_g = jnp.concatenate(parts, axis=1) + y_off + dskip_ref[0, :, ch] * xs[:, ch]
        y_g = y_g * _silu(z_pad_slice(z_ref, ch))
        ms = jnp.mean(jnp.square(y_g), axis=-1, keepdims=True)
        y_g = y_g * lax.rsqrt(ms + RMS_EPS) * ng_ref[0, :, ch]
        y_ref[0, :, ch] = y_g[:rows].astype(y_ref.dtype)
        states_t = _dot(bm_g.astype(F32).T.astype(BF16), xdt_end[:, ch])
        ht_ref[g] = h_prev * jnp.exp(cum_last_x[:, ch]) + states_t

    @pl.when(ci == n_chunks - 1)
    def _():
        for g in range(SSM_GROUPS):
            hout_ref[0, g * gw:(g + 1) * gw, :] = ht_ref[g].T


def z_pad_slice(z_ref, ch):
    return _pad_rows(z_ref[0, :, ch], CHUNK)


def _ssd_call(z3, xbc3, att3, conv_prev, h0, layer, params, valid_last, out_dtype, h0_layered):
    nb, t, _ = z3.shape
    rows = min(t, CHUNK)
    n_chunks = t // rows
    conv_w, conv_b, dt_bias, a_log, d_skip, norm_g = params
    hp = SSM_HEADS * SSM_HEAD_DIM
    if h0_layered:
        cprev_spec = pl.BlockSpec((1, None, CONV_W - 1, XBC_DIM), lambda b, c: (b, layer, 0, 0))
        h0_spec = pl.BlockSpec((1, None, hp, SSM_STATE), lambda b, c: (b, layer, 0, 0))
    else:
        cprev_spec = pl.BlockSpec((1, CONV_W - 1, XBC_DIM), lambda b, c: (b, 0, 0))
        h0_spec = pl.BlockSpec((1, hp, SSM_STATE), lambda b, c: (b, 0, 0))
    lvec = lambda w: pl.BlockSpec((1, 1, w), lambda b, c: (layer, 0, 0))
    return pl.pallas_call(
        functools.partial(_ssd_body, valid_last=valid_last, n_chunks=n_chunks),
        grid=(nb, n_chunks),
        in_specs=[pl.BlockSpec((1, rows, SSM_DIM), lambda b, c: (b, c, 0)),
                  pl.BlockSpec((1, rows, XBC_DIM), lambda b, c: (b, c, 0)),
                  pl.BlockSpec((1, rows, LANES), lambda b, c: (b, c, ATT_DTWI_BLK)),
                  cprev_spec, h0_spec,
                  pl.BlockSpec((1, 8, XBC_DIM), lambda b, c: (layer, 0, 0)),
                  lvec(XBC_DIM), lvec(LANES), lvec(LANES), lvec(SSM_DIM), lvec(SSM_DIM)],
        out_specs=[pl.BlockSpec((1, rows, SSM_DIM), lambda b, c: (b, c, 0)),
                   pl.BlockSpec((1, hp, SSM_STATE), lambda b, c: (b, 0, 0)),
                   pl.BlockSpec((1, CONV_W - 1, XBC_DIM), lambda b, c: (b, 0, 0))],
        out_shape=[jax.ShapeDtypeStruct((nb, t, SSM_DIM), out_dtype),
                   jax.ShapeDtypeStruct((nb, hp, SSM_STATE), F32),
                   jax.ShapeDtypeStruct((nb, CONV_W - 1, XBC_DIM), F32)],
        scratch_shapes=[pltpu.VMEM((8 + CHUNK, XBC_DIM), F32),
                        pltpu.VMEM((SSM_GROUPS, SSM_STATE, hp // SSM_GROUPS), F32)],
        compiler_params=_cp(2),
    )(z3, xbc3, att3, conv_prev, h0, conv_w, conv_b, dt_bias, a_log, d_skip, norm_g)


def _gelu_tanh(x):
    return 0.5 * x * (1.0 + jnp.tanh(math.sqrt(2.0 / math.pi) * (x + 0.044715 * (x * x * x))))


def _gmlp_body(gm_ref, lg_ref, lb_ref, ws_ref, bs_ref, o_ref, v_ref):
    rows = gm_ref.shape[1]
    g = _gelu_tanh(gm_ref[0])
    u = g[:, :GM_DIM]
    v = g[:, GM_DIM:]
    mu = jnp.mean(v, axis=-1, keepdims=True)
    var = jnp.mean(jnp.square(v - mu), axis=-1, keepdims=True)
    v = (v - mu) * lax.rsqrt(var + LN_EPS) * lg_ref[0] + lb_ref[0]
    v_ref[0] = v
    vb = _pad_rows(v, CHUNK).astype(BF16)
    causal = (lax.broadcasted_iota(I32, (CHUNK, CHUNK), 1) <= lax.broadcasted_iota(I32, (CHUNK, CHUNK), 0))
    gd = GM_DIM // GM_GROUPS
    bs = bs_ref[0]
    for grp in range(GM_GROUPS):
        sl = slice(grp * gd, (grp + 1) * gd)
        wm = jnp.where(causal, ws_ref[0, grp], 0.0).astype(BF16)
        mix = _dot(wm, vb[:, sl]) + bs[:, grp:grp + 1]
        o_ref[0, :, sl] = (u[:, sl] * mix[:rows]).astype(o_ref.dtype)


def _gmlp_call(gm3, layer, params, out_dtype):
    nb, t, _ = gm3.shape
    rows = min(t, CHUNK)
    ln_g, ln_b, ws, bs_t = params
    lvec = lambda w: pl.BlockSpec((1, 1, w), lambda b, c: (layer, 0, 0))
    return pl.pallas_call(
        _gmlp_body,
        grid=(nb, t // rows),
        in_specs=[pl.BlockSpec((1, rows, 2 * GM_DIM), lambda b, c: (b, c, 0)),
                  lvec(GM_DIM), lvec(GM_DIM),
                  pl.BlockSpec((1, GM_GROUPS, CHUNK, CHUNK), lambda b, c: (layer, 0, 0, 0)),
                  pl.BlockSpec((1, CHUNK, LANES), lambda b, c: (layer, 0, 0))],
        out_specs=[pl.BlockSpec((1, rows, GM_DIM), lambda b, c: (b, c, 0)),
                   pl.BlockSpec((1, rows, GM_DIM), lambda b, c: (b, c, 0))],
        out_shape=[jax.ShapeDtypeStruct((nb, t, GM_DIM), out_dtype),
                   jax.ShapeDtypeStruct((nb, t, GM_DIM), F32)],
        compiler_params=_cp(2),
    )(gm3, ln_g, ln_b, ws, bs_t)


def _merge_body(h_ref, oa_ref, ob_ref, oc_ref, wga_ref, wgb_ref, wgc_ref, wba_ref, wbb_ref, wbc_ref, m_ref):
    h = h_ref[...]
    acc = None
    for o_ref, wg_ref, wb_ref in ((oa_ref, wga_ref, wba_ref), (ob_ref, wgb_ref, wbb_ref), (oc_ref, wgc_ref, wbc_ref)):
        term = jax.nn.sigmoid(_dot(h, wg_ref[0])) * _dot(o_ref[...], wb_ref[0, 0])
        acc = term if acc is None else acc + term
    m_ref[...] = acc.astype(m_ref.dtype)


def _merge_call(h, oa, ob, oc, w_gate, w_branch, layer, tm, tn):
    n, d = h.shape
    nj = d // tn
    row = lambda w: pl.BlockSpec((tm, w), lambda i, j: (i, 0))
    wg = lambda br: pl.BlockSpec((1, d, tn), lambda i, j: (layer, 0, br * nj + j))
    wb = lambda br: pl.BlockSpec((1, 1, BRANCH_DIM, tn), lambda i, j: (layer, br, 0, j))
    return pl.pallas_call(
        _merge_body,
        grid=(n // tm, nj),
        in_specs=[row(d), row(BRANCH_DIM), row(BRANCH_DIM), row(BRANCH_DIM),
                  wg(0), wg(1), wg(2), wb(0), wb(1), wb(2)],
        out_specs=pl.BlockSpec((tm, tn), lambda i, j: (i, j)),
        out_shape=jax.ShapeDtypeStruct((n, d), BF16),
        compiler_params=_cp(2),
    )(h, oa, ob, oc, w_gate, w_gate, w_gate, w_branch, w_branch, w_branch)


def _ln_finish(acc_ref, nj, tn, lg_ref, lb_ref, sc_ref, sh_ref, xo_ref, ho_ref, po_ref):
    tm = acc_ref.shape[1]
    d = nj * tn
    s = jnp.zeros((tm, 1), F32)
    for jj in range(nj):
        s = s + jnp.sum(acc_ref[jj], axis=1, keepdims=True)
    mu = s / d
    v = jnp.zeros((tm, 1), F32)
    for jj in range(nj):
        dlt = acc_ref[jj] - mu
        v = v + jnp.sum(dlt * dlt, axis=1, keepdims=True)
    rstd = lax.rsqrt(v / d + LN_EPS)

    def finish(jj):
        sl = slice(jj * tn, (jj + 1) * tn)
        y = (acc_ref[jj] - mu) * rstd * lg_ref[0, :, sl] + lb_ref[0, :, sl]
        xo_ref[:, sl] = y
        hmod = y * (1.0 + sc_ref[0, :, sl]) + sh_ref[0, :, sl]
        hb = hmod.astype(BF16)
        ho_ref[:, sl] = hb
        return hb

    for jj in range(nj // 2):
        lo = finish(jj)
        hi = finish(jj + nj // 2)
        if po_ref is not None:
            _pack_store(po_ref, lo, hi, jj * (tn // LANES), tm, d // 2 // LANES)


def _pack_store(po_ref, lo, hi, group0, tm, groups):
    lo_bits = lax.shift_right_logical(lax.bitcast_convert_type(lo.astype(F32), I32), jnp.int32(16))
    hi_bits = lax.bitcast_convert_type(hi.astype(F32), I32) & jnp.int32(-65536)
    word = lo_bits | hi_bits
    for g in range(word.shape[1] // LANES):
        po_ref[pl.ds(group0 + g, tm, stride=groups), :] = word[:, g * LANES:(g + 1) * LANES]


def _projln_body(a_ref, w_ref, x_ref, g_ref, lg_ref, lb_ref, sc_ref, sh_ref, xo_ref, ho_ref, po_ref, acc_ref,
                 *, nj, tn, alpha):
    j = pl.program_id(1)
    acc_ref[j] = _dot(a_ref[...], w_ref[0])

    @pl.when(j == nj - 1)
    def _():
        for jj in range(nj):
            sl = slice(jj * tn, (jj + 1) * tn)
            acc_ref[jj] = alpha * x_ref[:, sl] + g_ref[0, :, sl] * acc_ref[jj]
        _ln_finish(acc_ref, nj, tn, lg_ref, lb_ref, sc_ref, sh_ref, xo_ref, ho_ref, po_ref)


def _projln_call(grp, a, w_out, x, gate, ln_g, ln_b, sc, sh, layer, alpha, tn):
    n, d = x.shape
    tm = grp.tm
    nj = d // tn
    row = lambda w: pl.BlockSpec((tm, w), lambda i, j: (i, 0))
    lvec = pl.BlockSpec((1, 1, d), lambda i, j: (layer, 0, 0))
    return pl.pallas_call(
        functools.partial(_projln_body, nj=nj, tn=tn, alpha=alpha),
        grid=(n // tm, nj),
        in_specs=[row(d), pl.BlockSpec((1, d, tn), lambda i, j: (layer, 0, j)), row(d),
                  grp.mod_spec(), lvec, lvec, grp.mod_spec(), grp.mod_spec()],
        out_specs=[row(d), row(d), pl.BlockSpec((tm * (d // 2 // LANES), LANES), lambda i, j: (i, 0))],
        out_shape=[jax.ShapeDtypeStruct((n, d), F32), jax.ShapeDtypeStruct((n, d), BF16),
                   jax.ShapeDtypeStruct((n * (d // 2 // LANES), LANES), I32)],
        scratch_shapes=[pltpu.VMEM((nj, tm, tn), F32)],
        compiler_params=_cp(2),
    )(a, w_out, x, gate, ln_g, ln_b, sc, sh)


def _combineln_body(y_ref, tw_ref, x_ref, g_ref, lg_ref, lb_ref, sc_ref, sh_ref, xo_ref, ho_ref, acc_ref,
                    *, nj, tn, alpha):
    tw = tw_ref[...]
    tm = x_ref.shape[0]
    groups = nj * tn // LANES

    def slot_cols(k, jj):
        g0 = jj * (tn // LANES)
        return jnp.concatenate([y_ref[k, pl.ds(g0 + g, tm, stride=groups), :] for g in range(tn // LANES)], axis=1)

    for jj in range(nj):
        sl = slice(jj * tn, (jj + 1) * tn)
        moe = slot_cols(0, jj) * tw[:, 0:1]
        for k in range(1, TOP_K):
            moe = moe + slot_cols(k, jj) * tw[:, k:k + 1]
        acc_ref[jj] = alpha * x_ref[:, sl] + g_ref[0, :, sl] * moe
    _ln_finish(acc_ref, nj, tn, lg_ref, lb_ref, sc_ref, sh_ref, xo_ref, ho_ref, None)


def _combineln_call(grp, y4, row0, tw, x, gate, ln_g, ln_b, sc, sh, layer, alpha, tm):
    n, d = x.shape
    tn = 512
    nj = d // tn
    rb = row0 // tm
    groups = d // LANES
    g2 = _Group(grp.n, tm, grp.rows_per_seq, grp.per_row)
    row = lambda w: pl.BlockSpec((tm, w), lambda i: (i, 0))
    lvec = pl.BlockSpec((1, 1, d), lambda i: (layer, 0, 0))
    return pl.pallas_call(
        functools.partial(_combineln_body, nj=nj, tn=tn, alpha=alpha),
        grid=(n // tm,),
        in_specs=[pl.BlockSpec((TOP_K, tm * groups, LANES), lambda i: (0, rb + i, 0)),
                  pl.BlockSpec((tm, LANES), lambda i: (rb + i, 0)), row(d),
                  g2.mod_spec(), lvec, lvec, g2.mod_spec(), g2.mod_spec()],
        out_specs=[row(d), row(d)],
        out_shape=[jax.ShapeDtypeStruct((n, d), F32), jax.ShapeDtypeStruct((n, d), BF16)],
        scratch_shapes=[pltpu.VMEM((nj, tm, tn), F32)],
        compiler_params=_cp(1),
    )(y4, tw, x, gate, ln_g, ln_b, sc, sh)


def _router_body(h_ref, w_ref, b_ref, ti_ref, tw_ref):
    lg = _dot(h_ref[...], w_ref[0]) + b_ref[0]
    tm = lg.shape[0]
    lane = lax.broadcasted_iota(I32, (tm, LANES), 1)
    lg = jnp.where(lane < N_EXPERTS, lg, -jnp.inf)
    vals, idxs = [], []
    for _ in range(TOP_K):
        m = jnp.max(lg, axis=1, keepdims=True)
        idx = jnp.min(jnp.where(lg == m, lane, LANES), axis=1, keepdims=True)
        vals.append(m)
        idxs.append(idx)
        lg = jnp.where(lane == idx, -jnp.inf, lg)
    es = [jnp.exp(v - vals[0]) for v in vals]
    tot = es[0] + es[1] + es[2] + es[3]
    ti = jnp.zeros((tm, LANES), I32)
    tw = jnp.zeros((tm, LANES), F32)
    for k in range(TOP_K):
        ti = jnp.where(lane == k, idxs[k], ti)
        tw = jnp.where(lane == k, es[k] / tot, tw)
    ti_ref[...] = ti
    tw_ref[...] = tw


def _router_call(h, w_router, b_router, layer, tm):
    n, d = h.shape
    return pl.pallas_call(
        _router_body,
        grid=(n // tm,),
        in_specs=[pl.BlockSpec((tm, d), lambda i: (i, 0)),
                  pl.BlockSpec((1, d, LANES), lambda i: (layer, 0, 0)),
                  pl.BlockSpec((1, 1, LANES), lambda i: (layer, 0, 0))],
        out_specs=[pl.BlockSpec((tm, LANES), lambda i: (i, 0)), pl.BlockSpec((tm, LANES), lambda i: (i, 0))],
        out_shape=[jax.ShapeDtypeStruct((n, LANES), I32), jax.ShapeDtypeStruct((n, LANES), F32)],
        compiler_params=_cp(1),
    )(h, w_router, b_router)


def _rank_body(ti_ref, rank_ref, cnt_ref, carry_ref):
    i = pl.program_id(0)

    @pl.when(i == 0)
    def _():
        carry_ref[...] = jnp.zeros_like(carry_ref)

    ti = ti_ref[...]
    tm = ti.shape[0]
    lane = lax.broadcasted_iota(I32, (tm, LANES), 1)
    onehot = jnp.zeros((tm, LANES), F32)
    for k in range(TOP_K):
        onehot = onehot + (lane == ti[:, k:k + 1]).astype(F32)
    strict = (lax.broadcasted_iota(I32, (tm, tm), 1) < lax.broadcasted_iota(I32, (tm, tm), 0)).astype(BF16)
    before = _dot(strict, onehot.astype(BF16)) + carry_ref[0:1, :]
    rank = jnp.zeros((tm, LANES), F32)
    for k in range(TOP_K):
        rk = jnp.sum(jnp.where(lane == ti[:, k:k + 1], before, 0.0), axis=1, keepdims=True)
        rank = jnp.where(lane == k, rk, rank)
    rank_ref[...] = rank.astype(I32)
    total = carry_ref[0:1, :] + jnp.sum(onehot, axis=0, keepdims=True)
    carry_ref[...] = jnp.broadcast_to(total, carry_ref.shape)
    cnt_ref[...] = jnp.broadcast_to(total, cnt_ref.shape).astype(I32)


def _rank_call(ti, tm):
    n = ti.shape[0]
    return pl.pallas_call(
        _rank_body,
        grid=(n // tm,),
        in_specs=[pl.BlockSpec((tm, LANES), lambda i: (i, 0))],
        out_specs=[pl.BlockSpec((tm, LANES), lambda i: (i, 0)), pl.BlockSpec((8, LANES), lambda i: (0, 0))],
        out_shape=[jax.ShapeDtypeStruct((n, LANES), I32), jax.ShapeDtypeStruct((8, LANES), I32)],
        scratch_shapes=[pltpu.VMEM((8, LANES), F32)],
        compiler_params=_cp(1),
    )(ti)


def _rowcopy_body(src_idx_ref, dst_idx_ref, src_ref, *rest, per_step, rows):
    dst_ref, sem = rest[-2:]

    def token_copy(s, d):
        return pltpu.make_async_copy(src_ref.at[pl.ds(pl.multiple_of(s * rows, rows), rows)],
                                     dst_ref.at[pl.ds(pl.multiple_of(d * rows, rows), rows)], sem)

    def issue(i, carry):
        token_copy(src_idx_ref[i], dst_idx_ref[i]).start()
        return carry

    lax.fori_loop(0, per_step, issue, 0, unroll=8)

    def wait(i, carry):
        token_copy(0, 0).wait()
        return carry

    lax.fori_loop(0, per_step, wait, 0, unroll=8)


def _rowcopy_call(src_idx, dst_idx, src, dst_tokens, rows, per_step, zero_fill):
    n_copies = src_idx.shape[0]
    smem = pl.BlockSpec((per_step,), lambda i: (i,), memory_space=pltpu.SMEM)
    any_spec = pl.BlockSpec(memory_space=pl.ANY)
    shape = (dst_tokens * rows, LANES)
    extra = (jnp.zeros(shape, src.dtype),) if zero_fill else ()
    return pl.pallas_call(
        functools.partial(_rowcopy_body, per_step=per_step, rows=rows),
        grid=(n_copies // per_step,),
        in_specs=[smem, smem, any_spec] + [any_spec] * len(extra),
        out_specs=any_spec,
        out_shape=jax.ShapeDtypeStruct(shape, src.dtype),
        scratch_shapes=[pltpu.SemaphoreType.DMA(())],
        input_output_aliases={3: 0} if zero_fill else {},
        compiler_params=pltpu.CompilerParams(dimension_semantics=("arbitrary",), has_side_effects=True,
                                             disable_bounds_checks=True),
    )(src_idx, dst_idx, src, *extra)


def _unpack(xp):
    lo = lax.bitcast_convert_type(lax.shift_left(xp, jnp.int32(16)), F32).astype(BF16)
    hi = lax.bitcast_convert_type(xp & jnp.int32(-65536), F32).astype(BF16)
    return lo, hi


def _expert1_body(blk_ref, e_ref, f_ref, wf_ref, first_ref, n_ref, xs_ref, wg_ref, wl_ref, bg_ref, bl_ref, act_ref,
                  wgb_ref, wlb_ref):
    s = pl.program_id(0)

    @pl.when(s < n_ref[0])
    def _():
        @pl.when(first_ref[s] == 1)
        def _():
            wgb_ref[...] = wg_ref[0, 0].astype(BF16)
            wlb_ref[...] = wl_ref[0, 0].astype(BF16)

        groups = xs_ref.shape[0] // MOE_TM
        words = jnp.concatenate([xs_ref[pl.ds(g, MOE_TM, stride=groups), :] for g in range(groups)], axis=1)
        lo, hi = _unpack(words)
        half = lo.shape[1]
        glu = _dot(lo, wgb_ref[0:half, :]) + _dot(hi, wgb_ref[half:, :]) + bg_ref[0, 0]
        lin = _dot(lo, wlb_ref[0:half, :]) + _dot(hi, wlb_ref[half:, :]) + bl_ref[0, 0]
        glu = jnp.minimum(glu, SWIGLU_LIMIT)
        lin = jnp.clip(lin, -SWIGLU_LIMIT, SWIGLU_LIMIT)
        act_ref[...] = (glu * jax.nn.sigmoid(SWIGLU_ALPHA * glu) * (lin + 1.0)).astype(BF16)

    @pl.when(s >= n_ref[0])
    def _():
        act_ref[...] = jnp.zeros_like(act_ref)


def _expert1_call(plan, xs, w1, b1, layer):
    d = w1.shape[2]
    groups = d // 2 // LANES
    cap = xs.shape[0] // groups
    n_steps_max = plan["step_blk"].shape[0]
    nf = EXPERT_DIM // MOE_TF
    grid_spec = pltpu.PrefetchScalarGridSpec(
        num_scalar_prefetch=6,
        grid=(n_steps_max,),
        in_specs=[pl.BlockSpec((MOE_TM * groups, LANES), lambda s, blk, e, f, wf, fi, n: (blk[s], 0)),
                  pl.BlockSpec((1, 1, d, MOE_TF), lambda s, blk, e, f, wf, fi, n: (layer, e[s], 0, wf[s])),
                  pl.BlockSpec((1, 1, d, MOE_TF), lambda s, blk, e, f, wf, fi, n: (layer, e[s], 0, nf + wf[s])),
                  pl.BlockSpec((1, 1, 1, MOE_TF), lambda s, blk, e, f, wf, fi, n: (layer, e[s], 0, wf[s])),
                  pl.BlockSpec((1, 1, 1, MOE_TF), lambda s, blk, e, f, wf, fi, n: (layer, e[s], 0, nf + wf[s]))],
        out_specs=pl.BlockSpec((MOE_TM, MOE_TF), lambda s, blk, e, f, wf, fi, n: (blk[s], f[s])),
        scratch_shapes=[pltpu.VMEM((d, MOE_TF), BF16), pltpu.VMEM((d, MOE_TF), BF16)],
    )
    return pl.pallas_call(
        _expert1_body,
        grid_spec=grid_spec,
        out_shape=jax.ShapeDtypeStruct((cap, EXPERT_DIM), BF16),
        compiler_params=_cp(1),
    )(plan["step_blk"], plan["step_e"], plan["step_f"], plan["step_wf"], plan["step_first"], plan["n_steps"],
      xs, w1, w1, b1, b1)


def _expert2_body(e_ref, first_ref, n_ref, act_ref, w_ref, b_ref, y_ref, wb_ref):
    i = pl.program_id(0)

    @pl.when(i < n_ref[0])
    def _():
        @pl.when(first_ref[i] == 1)
        def _():
            wb_ref[...] = w_ref[0, 0].astype(BF16)

        y = _dot(act_ref[...], wb_ref[...]) + b_ref[0, 0]
        groups = y.shape[1] // LANES
        for g in range(groups):
            y_ref[pl.ds(g, MOE_TM, stride=groups), :] = y[:, g * LANES:(g + 1) * LANES]

    @pl.when(i >= n_ref[0])
    def _():
        y_ref[...] = jnp.zeros_like(y_ref)


def _expert2_call(plan, act, w2, b2, layer):
    cap, f = act.shape
    d = w2.shape[-1]
    nblk = cap // MOE_TM
    grid_spec = pltpu.PrefetchScalarGridSpec(
        num_scalar_prefetch=3,
        grid=(nblk,),
        in_specs=[pl.BlockSpec((MOE_TM, f), lambda i, e, fi, n: (i, 0)),
                  pl.BlockSpec((1, 1, f, d), lambda i, e, fi, n: (layer, e[i], 0, 0)),
                  pl.BlockSpec((1, 1, 1, d), lambda i, e, fi, n: (layer, e[i], 0, 0))],
        out_specs=pl.BlockSpec((MOE_TM * (d // LANES), LANES), lambda i, e, fi, n: (i, 0)),
        scratch_shapes=[pltpu.VMEM((f, d), BF16)],
    )
    return pl.pallas_call(
        _expert2_body,
        grid_spec=grid_spec,
        out_shape=jax.ShapeDtypeStruct((cap * (d // LANES), LANES), F32),
        compiler_params=_cp(1),
    )(plan["blk_e"], plan["blk_first"], plan["n_blk"], act, w2, b2)


def _moe_plan(counts, top_i, rank, nblk_max):
    nb = (counts + MOE_TM - 1) // MOE_TM
    pad_end = jnp.cumsum(nb * MOE_TM)
    offsets = pad_end - nb * MOE_TM
    cumblk = jnp.cumsum(nb)
    blk_start = cumblk - nb
    n_blk = cumblk[-1]
    dest = jnp.take(offsets, top_i) + rank
    last = N_EXPERTS - 1
    bi = jnp.minimum(jnp.arange(nblk_max, dtype=I32), n_blk - 1)
    count_le = lambda ends, v: jnp.sum((ends[None, :] <= v[:, None]).astype(I32), axis=1)
    blk_e = jnp.minimum(count_le(cumblk, bi), last).astype(I32)
    blk_first = (bi == jnp.take(blk_start, blk_e)).astype(I32)
    nf = EXPERT_DIM // MOE_TF
    step_end = jnp.cumsum(nf * nb)
    n_steps = step_end[-1]
    s_all = jnp.arange(nf * nblk_max, dtype=I32)
    si = jnp.minimum(s_all, n_steps - 1)
    se = jnp.minimum(count_le(step_end, si), last).astype(I32)
    nbe = jnp.maximum(jnp.take(nb, se), 1)
    local = si - (jnp.take(step_end, se) - nf * jnp.take(nb, se))
    spare = s_all - n_steps
    swf = (local // nbe).astype(I32)
    sf = jnp.where(spare >= 0, spare % nf, swf).astype(I32)
    sblk = jnp.where(spare >= 0, n_blk + spare // nf, jnp.take(blk_start, se) + local % nbe).astype(I32)
    sfirst = (local % nbe == 0).astype(I32)
    return dict(dest=dest.astype(I32), blk_e=blk_e, blk_first=blk_first, n_blk=n_blk.reshape(1).astype(I32),
                step_blk=sblk, step_e=se, step_f=sf, step_wf=swf, step_first=sfirst,
                n_steps=n_steps.reshape(1).astype(I32))


def _moe(hp_all, ti_all, w1, b1, w2, b2, layer):
    n = ti_all.shape[0]
    d = w1.shape[2]
    rank_all, counts8 = _rank_call(ti_all, 64)
    counts = counts8[0, :N_EXPERTS]
    n_assign = n * TOP_K
    nblk_max = -(-(n_assign + N_EXPERTS * (MOE_TM - 1)) // MOE_TM)
    cap = nblk_max * MOE_TM
    plan = _moe_plan(counts, ti_all[:, :TOP_K], rank_all[:, :TOP_K], nblk_max)
    dest = plan["dest"].reshape(-1)
    tok = jnp.repeat(jnp.arange(n, dtype=I32), TOP_K)
    slot_row = (jnp.arange(TOP_K, dtype=I32)[None, :] * n + jnp.arange(n, dtype=I32)[:, None]).reshape(-1)
    per_step = 256
    xs = _rowcopy_call(tok, dest, hp_all, cap, d // 2 // LANES, per_step, True)
    act = _expert1_call(plan, xs, w1, b1, layer)
    ys = _expert2_call(plan, act, w2, b2, layer)
    y4 = _rowcopy_call(dest, slot_row, ys, TOP_K * n, d // LANES, per_step, False)
    return y4.reshape(TOP_K, n * (d // LANES), LANES)


def _mixers(grp, h, att, z, xbc, gm, nb, t, layer, prm, tables, tab_blocks, past, ssm_state, out_dtype):
    k_f, ki_f, kb, vb, kib = _kvpost_call(att, tables, tab_blocks, prm["idx_kn_g"], prm["idx_kn_b"], layer,
                                          min(grp.n, 512))
    att3 = att.reshape(nb, t, ATT_COLS)
    if past is None:
        nk = t
        q_pos0 = 0
        kb3, vb3, kib3 = (a.reshape(nb, t, -1) for a in (kb, vb, kib))
        tq = CHUNK
    else:
        page_table, cache_k, cache_v, cache_idx_k = past
        v_f = att[:, ATT_V_BLK * KV_DIM:(ATT_V_BLK + 1) * KV_DIM]
        kb3, vb3, kib3 = _pages_call(page_table, cache_k, cache_v, cache_idx_k, k_f.reshape(nb, t, KV_DIM),
                                     v_f.reshape(nb, t, KV_DIM), ki_f.reshape(nb, t, IDX_DIM), layer)
        q_pos0 = page_table.shape[1] * PAGE_SIZE
        nk = q_pos0 + prm["t_valid"]
        tq = t
    topk = min(TOPK_MAX, nk // 4)
    if past is None:
        n_qblk = t // tq
        per_call = 2 if n_qblk % 2 == 0 else 1
        o_a = jnp.concatenate(
            [_dsa_call(att3, tables, kb3, vb3, kib3, tq, q_pos0, topk, out_dtype, blk0=b0, n_blk=per_call,
                       nk=(b0 + per_call) * tq) for b0 in range(0, n_qblk, per_call)], axis=1)
    else:
        o_a = _dsa_call(att3, tables, kb3, vb3, kib3, tq, q_pos0, topk, out_dtype)
    ssm_params = (prm["conv_w"], prm["conv_b"], prm["dt_bias"], prm["a_log"], prm["d_skip"], prm["ssm_norm_g"])
    conv_prev, h0, layered = ssm_state
    valid_last = prm["t_valid"] if t < CHUNK else CHUNK
    o_b, ssm_new, conv_new = _ssd_call(z.reshape(nb, t, SSM_DIM), xbc.reshape(nb, t, XBC_DIM), att3, conv_prev, h0,
                                       layer, ssm_params, valid_last, out_dtype, layered)
    gm_params = (prm["gm_ln_g"], prm["gm_ln_b"], prm["gm_ws"], prm["gm_bs_t"])
    o_c, gm_v = _gmlp_call(gm.reshape(nb, t, 2 * GM_DIM), layer, gm_params, out_dtype)
    flat = lambda a: a.reshape(nb * t, -1).astype(BF16)
    return (flat(o_a), flat(o_b), flat(o_c)), (k_f, ki_f, ssm_new, conv_new, gm_v)


def kernel(x_prompt, x_sample, cache_k, cache_v, cache_idx_k, state_ssm, state_conv, page_table, c_prompt, c_sample, ln1_g, ln1_b, ln2_g, ln2_b, w_ada, b_ada, w_in, idx_kn_g, idx_kn_b, conv_w, conv_b, dt_bias, a_log, d_skip, ssm_norm_g, gm_ln_g, gm_ln_b, gm_ws, gm_bs, w_gate, w_branch, w_out, w_router, b_router, w1, b1, w2, b2):
    bp, sp, d = x_prompt.shape
    db, ds, _ = x_sample.shape
    depth = w_in.shape[0]
    alpha = (2 * depth) ** 0.25
    past_len = page_table.shape[1] * PAGE_SIZE
    n_p, n_s = bp * sp, db * SAMPLE_ROWS
    grp_p = _Group(n_p, 256, sp, False)
    grp_s = _Group(n_s, n_s, SAMPLE_ROWS, True)

    o = np.cumsum([0, ATT_DIM, KV_DIM, KV_DIM, IDX_HEADS * IDX_DIM, IDX_DIM, IDX_HEADS, SSM_DIM, XBC_DIM, SSM_HEADS,
                   2 * GM_DIM])
    seg = lambda i, j=None: w_in[:, :, o[i]:o[(i if j is None else j) + 1]].astype(BF16)
    pad_cols = LANES - SSM_HEADS - IDX_HEADS
    w_proj = jnp.concatenate([seg(6), seg(9), seg(7), seg(0, 4), seg(8), seg(5),
                              jnp.zeros((depth, d, pad_cols), BF16)], axis=-1)
    w_gate_b = w_gate.astype(BF16)
    w_branch_b = w_branch.astype(BF16)
    w_out_b = w_out.astype(BF16)
    w_router_b = jnp.pad(w_router, ((0, 0), (0, 0), (0, LANES - N_EXPERTS))).astype(BF16)
    lane_pad = lambda a: jnp.pad(a, ((0, 0), (0, LANES - a.shape[-1])))[:, None, :]
    vec = lambda a: a[:, None, :]
    prm = dict(
        idx_kn_g=vec(idx_kn_g), idx_kn_b=vec(idx_kn_b),
        conv_w=jnp.pad(conv_w, ((0, 0), (0, 8 - CONV_W), (0, 0))), conv_b=vec(conv_b),
        dt_bias=lane_pad(dt_bias), a_log=lane_pad(a_log),
        d_skip=vec(jnp.repeat(d_skip, SSM_HEAD_DIM, axis=-1)), ssm_norm_g=vec(ssm_norm_g),
        gm_ln_g=vec(gm_ln_g), gm_ln_b=vec(gm_ln_b), gm_ws=gm_ws,
        gm_bs_t=jnp.pad(jnp.swapaxes(gm_bs, 1, 2), ((0, 0), (0, 0), (0, LANES - GM_GROUPS))),
    )
    b_router_p = lane_pad(b_router)
    b1_4 = b1[:, :, None, :]
    b2_4 = b2[:, :, None, :]
    ln = dict(g1=vec(ln1_g), b1=vec(ln1_b), g2=vec(ln2_g), b2=vec(ln2_b))

    tab_p = _rope_tables(jnp.arange(sp, dtype=I32))
    pos_s = past_len + jnp.arange(SAMPLE_ROWS, dtype=I32)
    tab_s1 = _rope_tables(pos_s)
    tab_s = tuple(jnp.tile(a, (db, 1)) for a in tab_s1)

    n_c = bp + db
    c_rows = jnp.concatenate([c_prompt, c_sample, jnp.zeros((-n_c % 8, d), F32)], axis=0)
    mod = _mod_call(c_rows, w_ada, b_ada).reshape(depth, c_rows.shape[0], 6, d)

    def mods(layer, which, grp, lo, hi):
        return grp.mod_array(mod[layer, lo:hi, which])

    zero_mod_p = grp_p.mod_array(jnp.zeros((bp, d), F32))
    zero_mod_s = grp_s.mod_array(jnp.zeros((db, d), F32))

    xs_pad = jnp.pad(x_sample, ((0, 0), (0, SAMPLE_ROWS - ds), (0, 0))).reshape(n_s, d)
    xp = x_prompt.reshape(n_p, d)
    xs_ = xs_pad
    hp = _modulate_call(grp_p, xp, mods(0, 1, grp_p, 0, bp), mods(0, 0, grp_p, 0, bp))
    hs = _modulate_call(grp_s, xs_, mods(0, 1, grp_s, bp, n_c), mods(0, 0, grp_s, bp, n_c))

    conv0 = jnp.zeros((bp, CONV_W - 1, XBC_DIM), F32)
    ssm0 = jnp.zeros((bp, SSM_HEADS * SSM_HEAD_DIM, SSM_STATE), F32)
    state_ssm4 = state_ssm.reshape(db, depth, SSM_HEADS * SSM_HEAD_DIM, SSM_STATE)

    outs_p = [[] for _ in range(5)]
    outs_s = [[] for _ in range(6)]
    for layer in range(depth):
        groups = (
            (grp_p, hp, xp, bp, sp, tab_p, sp // min(n_p, 512), None, (conv0, ssm0, False), BF16, min(n_p, 1024),
             0, bp),
            (grp_s, hs, xs_, db, SAMPLE_ROWS, tab_s, 1, (page_table, cache_k, cache_v, cache_idx_k),
             (state_conv, state_ssm4, True), F32, n_s, bp, n_c),
        )
        x1s, h2s, packs, tis, tws = [], [], [], [], []
        for grp, h, x, nb, t, tabs, tab_blocks, past, ssm_state, odt, tm_mm, lo, hi in groups:
            z = _mm_call(h, w_proj, layer, COL_Z, SSM_DIM, tm_mm, 512)
            gm = _mm_call(h, w_proj, layer, COL_GM, 2 * GM_DIM, tm_mm, 512)
            xbc = _mm_call(h, w_proj, layer, COL_XBC, XBC_DIM, tm_mm, 512)
            att = _mm_call(h, w_proj, layer, COL_ATT, ATT_COLS, tm_mm, 256)
            prm_g = dict(prm, t_valid=(t if past is None else ds))
            branches, new = _mixers(grp, h, att, z, xbc, gm, nb, t, layer, prm_g, tabs, tab_blocks, past, ssm_state, odt)
            k_f, ki_f, ssm_new, conv_new, gm_v = new
            v_f = att[:, ATT_V_BLK * KV_DIM:(ATT_V_BLK + 1) * KV_DIM]
            if past is None:
                outs_p[0].append(k_f.reshape(nb, t, N_KV_HEADS, HEAD_DIM))
                outs_p[1].append(v_f.reshape(nb, t, N_KV_HEADS, HEAD_DIM))
                outs_p[2].append(ki_f.reshape(nb, t, IDX_DIM))
                outs_p[3].append(ssm_new.reshape(nb, SSM_HEADS, SSM_HEAD_DIM, SSM_STATE))
                outs_p[4].append(conv_new)
            else:
                outs_s[0].append(k_f.reshape(nb, t, N_KV_HEADS, HEAD_DIM)[:, :ds])
                outs_s[1].append(v_f.reshape(nb, t, N_KV_HEADS, HEAD_DIM)[:, :ds])
                outs_s[2].append(ki_f.reshape(nb, t, IDX_DIM)[:, :ds])
                outs_s[3].append(ssm_new.reshape(nb, SSM_HEADS, SSM_HEAD_DIM, SSM_STATE))
                outs_s[4].append(conv_new)
                outs_s[5].append(gm_v[:, :ds])
            merged = _merge_call(h, *branches, w_gate_b, w_branch_b, layer, min(grp.n, 512), 256)
            x1, h2, pack = _projln_call(grp, merged, w_out_b, x, mods(layer, 2, grp, lo, hi), ln["g1"], ln["b1"],
                                        mods(layer, 4, grp, lo, hi), mods(layer, 3, grp, lo, hi), layer, alpha, 512)
            ti, tw = _router_call(h2, w_router_b, b_router_p, layer, min(grp.n, 512))
            x1s.append(x1)
            packs.append(pack)
            tis.append(ti)
            tws.append(tw)
        hp_all = jnp.concatenate(packs, axis=0)
        ti_all = jnp.concatenate(tis, axis=0)
        tw_all = jnp.concatenate(tws, axis=0)
        y4 = _moe(hp_all, ti_all, w1, b1_4, w2, b2_4, layer)
        nxt = layer + 1
        new_x, new_h = [], []
        for (grp, lo, hi, zero_mod, row0, tm), x1 in zip(
                ((grp_p, 0, bp, zero_mod_p, 0, 128), (grp_s, bp, n_c, zero_mod_s, n_p, n_s)), x1s):
            sc = mods(nxt, 1, grp, lo, hi) if nxt < depth else zero_mod
            sh = mods(nxt, 0, grp, lo, hi) if nxt < depth else zero_mod
            xo, ho = _combineln_call(grp, y4, row0, tw_all, x1, mods(layer, 5, grp, lo, hi), ln["g2"], ln["b2"],
                                     sc, sh, layer, alpha, tm)
            new_x.append(xo)
            new_h.append(ho)
        (xp, xs_), (hp, hs) = new_x, new_h

    stack = lambda lst: jnp.stack(lst, axis=1)
    y_prompt = xp.reshape(bp, sp, d)
    y_sample = xs_.reshape(db, SAMPLE_ROWS, d)[:, :ds]
    return (y_prompt, y_sample, *(stack(a) for a in outs_p), *(stack(a) for a in outs_s))
```

```python
import functools
import math

import jax
import jax.numpy as jnp
import numpy as np
from jax import lax
from jax.experimental import pallas as pl
from jax.experimental.pallas import tpu as pltpu

F32 = jnp.float32
BF16 = jnp.bfloat16
I32 = jnp.int32

D_MODEL = 4096
PAGE_SIZE = 128
N_HEADS = 16
HEAD_DIM = 128
N_KV_HEADS = 4
Q_PER_KV = N_HEADS // N_KV_HEADS
ATT_DIM = N_HEADS * HEAD_DIM
KV_DIM = N_KV_HEADS * HEAD_DIM
ROT_DIM = HEAD_DIM // 4
IDX_HEADS = 8
IDX_DIM = 128
TOPK_MAX = 256
ROPE_THETA = 500000.0
ATT_SCALE = HEAD_DIM ** -0.5
IDX_SCALE = (IDX_DIM * IDX_HEADS) ** -0.5
SSM_DIM = 2048
SSM_HEAD_DIM = 64
SSM_HEADS = SSM_DIM // SSM_HEAD_DIM
SSM_GROUPS = 4
SSM_STATE = 128
CONV_W = 4
XBC_DIM = SSM_DIM + 2 * SSM_GROUPS * SSM_STATE
GM_DIM = 2048
GM_GROUPS = 16
N_BRANCH = 3
BRANCH_DIM = 2048
N_EXPERTS = 32
TOP_K = 4
EXPERT_DIM = 1024
SWIGLU_ALPHA = 1.702
SWIGLU_LIMIT = 7.0
LN_EPS = 1e-5
RMS_EPS = 1e-5

CHUNK = 128
SAMPLE_ROWS = 8
LANES = 128
COL_Z = 0
COL_GM = COL_Z + SSM_DIM
COL_XBC = COL_GM + 2 * GM_DIM
COL_ATT = COL_XBC + XBC_DIM
ATT_COLS = ATT_DIM + 2 * KV_DIM + IDX_HEADS * IDX_DIM + IDX_DIM + LANES
PROJ_COLS = COL_ATT + ATT_COLS
ATT_K_BLK = ATT_DIM // KV_DIM
ATT_V_BLK = ATT_K_BLK + 1
ATT_QI_BLK = (ATT_DIM + 2 * KV_DIM) // (IDX_HEADS * IDX_DIM)
ATT_KI_BLK = (ATT_DIM + 2 * KV_DIM + IDX_HEADS * IDX_DIM) // LANES
ATT_DTWI_BLK = ATT_KI_BLK + 1
WI_LANE = SSM_HEADS
MOE_TM = 256
MOE_TF = 512
VMEM_LIMIT = 56 * 1024 * 1024


def _cp(n_axes, vmem=VMEM_LIMIT):
    return pltpu.CompilerParams(dimension_semantics=("arbitrary",) * n_axes, vmem_limit_bytes=vmem)


def _dot(a, b):
    return jnp.dot(a, b, preferred_element_type=F32)


def _dot_nt(a, b):
    return lax.dot_general(a, b, (((1,), (1,)), ((), ())), preferred_element_type=F32)


def _split3(x):
    hi = x.astype(BF16)
    r1 = x - hi.astype(F32)
    mid = r1.astype(BF16)
    lo = (r1 - mid.astype(F32)).astype(BF16)
    return hi, mid, lo


def _dot3_left(m_bf16, x):
    hi, mid, lo = _split3(x)
    return _dot(m_bf16, hi) + _dot(m_bf16, mid) + _dot(m_bf16, lo)


def _dot3_right(x, m_bf16):
    hi, mid, lo = _split3(x)
    return _dot(hi, m_bf16) + _dot(mid, m_bf16) + _dot(lo, m_bf16)


def _silu(x):
    return x * jax.nn.sigmoid(x)


def _rope(x, c, s1, s2):
    half = ROT_DIM // 2
    return x * c + pltpu.roll(x, LANES - half, 1) * s1 + pltpu.roll(x, half, 1) * s2


def _rope_tables(pos):
    half = ROT_DIM // 2
    inv = ROPE_THETA ** (-jnp.arange(half, dtype=F32) / half)
    ang = pos.astype(F32)[:, None] * inv
    cos, sin = jnp.cos(ang), jnp.sin(ang)
    n = pos.shape[0]
    rest = LANES - ROT_DIM
    c = jnp.concatenate([cos, cos, jnp.ones((n, rest), F32)], axis=1)
    s1 = jnp.concatenate([-sin, jnp.zeros((n, LANES - half), F32)], axis=1)
    s2 = jnp.concatenate([jnp.zeros((n, half), F32), sin, jnp.zeros((n, rest), F32)], axis=1)
    return c, s1, s2


def _mod_body(c_ref, w_ref, b_ref, o_ref):
    s = _silu(c_ref[...]).astype(BF16)
    o_ref[0] = _dot(s, w_ref[0].astype(BF16)) + b_ref[0]


def _mod_call(c_rows, w_ada, b_ada):
    n_layers, d, n = w_ada.shape
    r = c_rows.shape[0]
    tn = 512
    return pl.pallas_call(
        _mod_body,
        grid=(n_layers, n // tn),
        in_specs=[pl.BlockSpec((r, d), lambda l, j: (0, 0)),
                  pl.BlockSpec((1, d, tn), lambda l, j: (l, 0, j)),
                  pl.BlockSpec((1, 1, tn), lambda l, j: (l, 0, j))],
        out_specs=pl.BlockSpec((1, r, tn), lambda l, j: (l, 0, j)),
        out_shape=jax.ShapeDtypeStruct((n_layers, r, n), F32),
        compiler_params=_cp(2),
    )(c_rows, w_ada, b_ada.reshape(n_layers, 1, n))


class _Group:
    def __init__(self, n, tm, rows_per_seq, per_row):
        self.n, self.tm, self.rows_per_seq, self.per_row = n, tm, rows_per_seq, per_row

    def mod_array(self, m):
        if self.per_row:
            return jnp.repeat(m, self.rows_per_seq, axis=0)[None]
        return m[:, None, :]

    def mod_spec(self, width=D_MODEL):
        if self.per_row:
            return pl.BlockSpec((1, self.n, width), lambda i, *_: (0, 0, 0))
        tm, rps = self.tm, self.rows_per_seq
        return pl.BlockSpec((1, 1, width), lambda i, *_: ((i * tm) // rps, 0, 0))


def _modulate_body(x_ref, sc_ref, sh_ref, h_ref):
    h_ref[...] = (x_ref[...] * (1.0 + sc_ref[0]) + sh_ref[0]).astype(BF16)


def _modulate_call(grp, x, sc, sh):
    n, d = x.shape
    tm = grp.tm
    return pl.pallas_call(
        _modulate_body,
        grid=(n // tm,),
        in_specs=[pl.BlockSpec((tm, d), lambda i: (i, 0)), grp.mod_spec(), grp.mod_spec()],
        out_specs=pl.BlockSpec((tm, d), lambda i: (i, 0)),
        out_shape=jax.ShapeDtypeStruct((n, d), BF16),
        compiler_params=_cp(1),
    )(x, sc, sh)


def _mm_body(a_ref, w_ref, o_ref):
    o_ref[...] = _dot(a_ref[...], w_ref[0]).astype(o_ref.dtype)


def _mm_call(a, w, layer, col0, ncols, tm, tn, out_dtype=F32):
    n, k = a.shape
    cb = col0 // tn
    return pl.pallas_call(
        _mm_body,
        grid=(n // tm, ncols // tn),
        in_specs=[pl.BlockSpec((tm, k), lambda i, j: (i, 0)),
                  pl.BlockSpec((1, k, tn), lambda i, j: (layer, 0, cb + j))],
        out_specs=pl.BlockSpec((tm, tn), lambda i, j: (i, j)),
        out_shape=jax.ShapeDtypeStruct((n, ncols), out_dtype),
        compiler_params=_cp(2),
    )(a, w)


def _kvpost_body(k_ref, v_ref, ki_ref, c_ref, s1_ref, s2_ref, g_ref, b_ref,
                 ko_ref, kio_ref, kb_ref, vb_ref, kib_ref):
    c, s1, s2 = c_ref[...], s1_ref[...], s2_ref[...]
    for h in range(N_KV_HEADS):
        sl = slice(h * HEAD_DIM, (h + 1) * HEAD_DIM)
        kh = _rope(k_ref[:, sl], c, s1, s2)
        ko_ref[:, sl] = kh
        kb_ref[:, sl] = kh.astype(BF16)
    vb_ref[...] = v_ref[...].astype(BF16)
    ki = ki_ref[...]
    mu = jnp.mean(ki, axis=-1, keepdims=True)
    var = jnp.mean(jnp.square(ki - mu), axis=-1, keepdims=True)
    kin = (ki - mu) * lax.rsqrt(var + LN_EPS) * g_ref[0] + b_ref[0]
    kir = _rope(kin, c, s1, s2)
    kio_ref[...] = kir
    kib_ref[...] = kir.astype(BF16)


def _kvpost_call(att, tables, tab_blocks, g, b, layer, tm):
    n = att.shape[0]
    tab = pl.BlockSpec((tm, LANES), lambda i: (i % tab_blocks, 0))
    vec = pl.BlockSpec((1, 1, LANES), lambda i: (layer, 0, 0))
    row = lambda w: pl.BlockSpec((tm, w), lambda i: (i, 0))
    return pl.pallas_call(
        _kvpost_body,
        grid=(n // tm,),
        in_specs=[pl.BlockSpec((tm, KV_DIM), lambda i: (i, ATT_K_BLK)),
                  pl.BlockSpec((tm, KV_DIM), lambda i: (i, ATT_V_BLK)),
                  pl.BlockSpec((tm, IDX_DIM), lambda i: (i, ATT_KI_BLK)),
                  tab, tab, tab, vec, vec],
        out_specs=[row(KV_DIM), row(IDX_DIM), row(KV_DIM), row(KV_DIM), row(IDX_DIM)],
        out_shape=[jax.ShapeDtypeStruct((n, KV_DIM), F32), jax.ShapeDtypeStruct((n, IDX_DIM), F32),
                   jax.ShapeDtypeStruct((n, KV_DIM), BF16), jax.ShapeDtypeStruct((n, KV_DIM), BF16),
                   jax.ShapeDtypeStruct((n, IDX_DIM), BF16)],
        compiler_params=_cp(1),
    )(att, att, att, *tables, g, b)


def _pages_body(pt_ref, *refs, n_steps, pps):
    ck_refs, cv_refs, ci_refs = refs[0:pps], refs[pps:2 * pps], refs[2 * pps:3 * pps]
    kn_ref, vn_ref, in_ref, ko_ref, vo_ref, io_ref = refs[3 * pps:]
    p = pl.program_id(1)

    @pl.when(p < n_steps)
    def _():
        for j in range(pps):
            rows = slice(j * PAGE_SIZE, (j + 1) * PAGE_SIZE)
            for h in range(N_KV_HEADS):
                sl = slice(h * HEAD_DIM, (h + 1) * HEAD_DIM)
                ko_ref[0, rows, sl] = ck_refs[j][:, h, :].astype(BF16)
                vo_ref[0, rows, sl] = cv_refs[j][:, h, :].astype(BF16)
            io_ref[0, rows, :] = ci_refs[j][...].astype(BF16)

    @pl.when(p == n_steps)
    def _():
        r = kn_ref.shape[1]
        tail = pps * PAGE_SIZE - r
        ko_ref[0] = jnp.concatenate([kn_ref[0], jnp.zeros((tail, KV_DIM), F32)], axis=0).astype(BF16)
        vo_ref[0] = jnp.concatenate([vn_ref[0], jnp.zeros((tail, KV_DIM), F32)], axis=0).astype(BF16)
        io_ref[0] = jnp.concatenate([in_ref[0], jnp.zeros((tail, IDX_DIM), F32)], axis=0).astype(BF16)


def _pages_call(page_table, cache_k, cache_v, cache_idx_k, k_new, v_new, ki_new, layer):
    nb, n_pages = page_table.shape
    r = k_new.shape[1]
    pps = math.gcd(n_pages, 8)
    n_steps = n_pages // pps
    blk = pps * PAGE_SIZE
    nk = (n_steps + 1) * blk

    def page_map(j, extra):
        def index_map(b, p, pt):
            return (pt[b * n_pages + jnp.minimum(p, n_steps - 1) * pps + j], layer) + (0,) * extra
        return index_map

    kv_spec = lambda j: pl.BlockSpec((None, None, PAGE_SIZE, N_KV_HEADS, HEAD_DIM), page_map(j, 3))
    ix_spec = lambda j: pl.BlockSpec((None, None, PAGE_SIZE, IDX_DIM), page_map(j, 2))
    new_spec = lambda w: pl.BlockSpec((1, r, w), lambda b, p, pt: (b, 0, 0))
    out_spec = lambda w: pl.BlockSpec((1, blk, w), lambda b, p, pt: (b, p, 0))
    grid_spec = pltpu.PrefetchScalarGridSpec(
        num_scalar_prefetch=1,
        grid=(nb, n_steps + 1),
        in_specs=([kv_spec(j) for j in range(pps)] + [kv_spec(j) for j in range(pps)]
                  + [ix_spec(j) for j in range(pps)] + [new_spec(KV_DIM), new_spec(KV_DIM), new_spec(IDX_DIM)]),
        out_specs=[out_spec(KV_DIM), out_spec(KV_DIM), out_spec(IDX_DIM)],
    )
    return pl.pallas_call(
        functools.partial(_pages_body, n_steps=n_steps, pps=pps),
        grid_spec=grid_spec,
        out_shape=[jax.ShapeDtypeStruct((nb, nk, KV_DIM), BF16), jax.ShapeDtypeStruct((nb, nk, KV_DIM), BF16),
                   jax.ShapeDtypeStruct((nb, nk, IDX_DIM), BF16)],
        compiler_params=_cp(2),
    )(page_table.reshape(-1), *([cache_k] * pps), *([cache_v] * pps), *([cache_idx_k] * pps), k_new, v_new, ki_new)


def _sort_key(score):
    bits = lax.bitcast_convert_type(score, I32)
    return jnp.where(bits < 0, bits ^ jnp.int32(0x7FFFFFFF), bits)


def _kth_largest(key, k):
    kf = jnp.float32(k)

    def count_ge(c):
        return jnp.sum((key >= c).astype(F32), axis=1, keepdims=True)

    int_min = jnp.int32(-2 ** 31)
    prefix = jnp.where(count_ge(jnp.int32(0)) >= kf, jnp.int32(0), int_min)

    def body(i, prefix):
        cand = prefix | lax.shift_left(jnp.int32(1), jnp.int32(30) - i)
        return jnp.where(count_ge(cand) >= kf, cand, prefix)

    return lax.fori_loop(0, 31, body, prefix)


def _select_topk(key, kpos, k):
    thr = _kth_largest(key, k)
    gt = key > thr
    eq = key == thr
    need = jnp.float32(k) - jnp.sum(gt.astype(F32), axis=1, keepdims=True)
    nbits = int(math.ceil(math.log2(key.shape[1] + 1)))

    def body(i, bound):
        cand = bound | lax.shift_left(jnp.int32(1), jnp.int32(nbits - 1) - i)
        cnt = jnp.sum((eq & (kpos < cand)).astype(F32), axis=1, keepdims=True)
        return jnp.where(cnt <= need, cand, bound)

    n_eq = jnp.sum(eq.astype(F32), axis=1, keepdims=True)
    tied = jnp.max(n_eq - need) > 0.0
    full = jnp.full_like(thr, 2 ** nbits - 1)
    bound = lax.cond(tied, lambda: lax.fori_loop(0, nbits, body, jnp.zeros_like(thr)), lambda: full)
    return gt | (eq & (kpos < bound))


def _dsa_body(q_ref, qi_ref, dtwi_ref, c_ref, s1_ref, s2_ref, k_ref, v_ref, ki_ref, o_ref, *, q_pos0, topk):
    tq = q_ref.shape[1]
    nk = k_ref.shape[1]
    c, s1, s2 = c_ref[...], s1_ref[...], s2_ref[...]
    ki = ki_ref[0]
    dtwi = dtwi_ref[0]
    score = jnp.zeros((tq, nk), F32)
    for h in range(IDX_HEADS):
        qh = _rope(qi_ref[0, :, h * IDX_DIM:(h + 1) * IDX_DIM], c, s1, s2).astype(BF16)
        d = _dot_nt(qh, ki)
        score = score + jnp.maximum(d, 0.0) * dtwi[:, WI_LANE + h:WI_LANE + h + 1]
    score = score * IDX_SCALE
    kpos = lax.broadcasted_iota(I32, (tq, nk), 1)
    qpos = q_pos0 + pl.program_id(1) * tq + lax.broadcasted_iota(I32, (tq, nk), 0)
    visible = kpos <= qpos
    score = jnp.where(visible, score, -jnp.inf)
    sel = _select_topk(_sort_key(score), kpos, topk) & visible
    bias = jnp.where(sel, 0.0, -jnp.inf)[None]
    for kvh in range(N_KV_HEADS):
        heads = [kvh * Q_PER_KV + g for g in range(Q_PER_KV)]
        qh = jnp.concatenate(
            [_rope(q_ref[0, :, h * HEAD_DIM:(h + 1) * HEAD_DIM], c, s1, s2) for h in heads], axis=0).astype(BF16)
        sl = slice(kvh * HEAD_DIM, (kvh + 1) * HEAD_DIM)
        logits = _dot_nt(qh, k_ref[0, :, sl]) * ATT_SCALE
        logits = logits.reshape(Q_PER_KV, tq, nk) + bias
        m = jnp.max(logits, axis=-1, keepdims=True)
        p = jnp.exp(logits - m)
        denom = jnp.sum(p, axis=-1, keepdims=True)
        o = _dot(p.astype(BF16).reshape(Q_PER_KV * tq, nk), v_ref[0, :, sl])
        o = o.reshape(Q_PER_KV, tq, HEAD_DIM) / denom
        for g, h in enumerate(heads):
            o_ref[0, :, h * HEAD_DIM:(h + 1) * HEAD_DIM] = o[g].astype(o_ref.dtype)


def _dsa_call(att3, tables, kb, vb, kib, tq, q_pos0, topk, out_dtype, blk0=0, n_blk=None, nk=None):
    nb, tq_total, _ = att3.shape
    nk = kb.shape[1] if nk is None else nk
    n_blk = tq_total // tq if n_blk is None else n_blk
    tab = pl.BlockSpec((tq, LANES), lambda b, j: (blk0 + j, 0))
    return pl.pallas_call(
        functools.partial(_dsa_body, q_pos0=q_pos0 + blk0 * tq, topk=topk),
        grid=(nb, n_blk),
        in_specs=[pl.BlockSpec((1, tq, ATT_DIM), lambda b, j: (b, blk0 + j, 0)),
                  pl.BlockSpec((1, tq, IDX_HEADS * IDX_DIM), lambda b, j: (b, blk0 + j, ATT_QI_BLK)),
                  pl.BlockSpec((1, tq, LANES), lambda b, j: (b, blk0 + j, ATT_DTWI_BLK)),
                  tab, tab, tab,
                  pl.BlockSpec((1, nk, KV_DIM), lambda b, j: (b, 0, 0)),
                  pl.BlockSpec((1, nk, KV_DIM), lambda b, j: (b, 0, 0)),
                  pl.BlockSpec((1, nk, IDX_DIM), lambda b, j: (b, 0, 0))],
        out_specs=pl.BlockSpec((1, tq, ATT_DIM), lambda b, j: (b, j, 0)),
        out_shape=jax.ShapeDtypeStruct((nb, n_blk * tq, ATT_DIM), out_dtype),
        compiler_params=_cp(2),
    )(att3, att3, att3, *tables, kb, vb, kib)


def _pad_rows(x, rows):
    r = x.shape[0]
    if r == rows:
        return x
    return jnp.concatenate([x, jnp.zeros((rows - r,) + x.shape[1:], x.dtype)], axis=0)


def _ssd_body(z_ref, xbc_ref, dtwi_ref, cprev_ref, h0_ref, cw_ref, cb_ref, dtb_ref, alog_ref, dskip_ref, ng_ref,
              y_ref, hout_ref, cnew_ref, xp_ref, ht_ref, *, valid_last, n_chunks):
    ci = pl.program_id(1)
    rows = xbc_ref.shape[1]
    e_heads = SSM_HEADS // SSM_GROUPS
    gw = e_heads * SSM_HEAD_DIM
    pre = 8

    @pl.when(ci == 0)
    def _():
        xp_ref[0:pre, :] = _pad_rows(jnp.concatenate(
            [jnp.zeros((pre - (CONV_W - 1), XBC_DIM), F32), cprev_ref[0]], axis=0), pre)
        for g in range(SSM_GROUPS):
            ht_ref[g] = h0_ref[0, g * gw:(g + 1) * gw, :].T

    xp_ref[pre:pre + CHUNK, :] = _pad_rows(xbc_ref[0], CHUNK)
    conv = cb_ref[0]
    for i in range(CONV_W):
        off = pre - (CONV_W - 1) + i
        conv = conv + xp_ref[off:off + CHUNK, :] * cw_ref[0, i:i + 1, :]
    last = valid_last if n_chunks == 1 else CHUNK

    @pl.when(ci == n_chunks - 1)
    def _():
        cnew_ref[0] = xp_ref[pre + valid_last - (CONV_W - 1):pre + valid_last, :]

    if n_chunks > 1:
        tail = xp_ref[pre + last - (CONV_W - 1):pre + last, :]
        xp_ref[pre - (CONV_W - 1):pre, :] = tail

    u = _silu(conv)
    xs = u[:, :SSM_DIM]
    bm = u[:, SSM_DIM:SSM_DIM + SSM_GROUPS * SSM_STATE]
    cm = u[:, SSM_DIM + SSM_GROUPS * SSM_STATE:]

    dt_raw = _pad_rows(dtwi_ref[0], CHUNK)
    lane = lax.broadcasted_iota(I32, (CHUNK, LANES), 1)
    rowi = lax.broadcasted_iota(I32, (CHUNK, LANES), 0)
    xdt_in = dt_raw + dtb_ref[0]
    dt = jnp.maximum(xdt_in, 0.0) + jnp.log1p(jnp.exp(-jnp.abs(xdt_in)))
    row_ok = rowi < jnp.where(ci == n_chunks - 1, valid_last, CHUNK)
    dt = jnp.where((lane < SSM_HEADS) & row_ok, dt, 0.0)
    a = -jnp.exp(alog_ref[0])
    da = dt * a
    tri = (lax.broadcasted_iota(I32, (CHUNK, CHUNK), 1) <= lax.broadcasted_iota(I32, (CHUNK, CHUNK), 0))
    cum = _dot3_left(tri.astype(BF16), da)
    cum_t = cum.T
    hh = lax.broadcasted_iota(I32, (LANES, SSM_DIM), 0)
    cc = lax.broadcasted_iota(I32, (LANES, SSM_DIM), 1)
    expand = (cc // SSM_HEAD_DIM == hh).astype(BF16)
    dt_x = _dot3_right(dt, expand)
    cum_x = _dot3_right(cum, expand)
    cum_last = cum[CHUNK - 1:CHUNK, :]
    cum_last_x = cum_x[CHUNK - 1:CHUNK, :]
    xdt = xs * dt_x
    grow = jnp.exp(cum_x)
    xdt_end = (xdt * jnp.exp(cum_last_x - cum_x)).astype(BF16)
    xdt_b = xdt.astype(BF16)
    lane_lo = lax.broadcasted_iota(I32, (CHUNK, LANES), 1) < SSM_HEAD_DIM

    for g in range(SSM_GROUPS):
        st = slice(g * SSM_STATE, (g + 1) * SSM_STATE)
        ch = slice(g * gw, (g + 1) * gw)
        bm_g = bm[:, st].astype(BF16)
        cm_g = cm[:, st].astype(BF16)
        cb = _dot_nt(cm_g, bm_g)
        h_prev = ht_ref[g]
        y_off = _dot(cm_g, h_prev.astype(BF16)) * grow[:, ch]
        parts = []
        for pr in range(e_heads // 2):
            acc = None
            pair = slice(g * gw + pr * LANES, g * gw + (pr + 1) * LANES)
            xp_pair = xdt_b[:, pair]
            for half in range(2):
                h = g * e_heads + pr * 2 + half
                seg = cum[:, h:h + 1] - cum_t[h:h + 1, :]
                dec = jnp.exp(jnp.where(tri, seg, -jnp.inf))
                m = (cb * dec).astype(BF16)
                keep = lane_lo if half == 0 else jnp.logical_not(lane_lo)
                part = _dot(m, jnp.where(keep, xp_pair, jnp.zeros_like(xp_pair)))
                acc = part if acc is None else acc + part
            parts.append(acc)
        y_g = jnp.concatenate(parts, axis=1) + y_off + dskip_ref[0, :, ch] * xs[:, ch]
        y_g = y_g * _silu(z_pad_slice(z_ref, ch))
        ms = jnp.mean(jnp.square(y_g), axis=-1, keepdims=True)
        y_g = y_g * lax.rsqrt(ms + RMS_EPS) * ng_ref[0, :, ch]
        y_ref[0, :, ch] = y_g[:rows].astype(y_ref.dtype)
        states_t = _dot(bm_g.astype(F32).T.astype(BF16), xdt_end[:, ch])
        ht_ref[g] = h_prev * jnp.exp(cum_last_x[:, ch]) + states_t

    @pl.when(ci == n_chunks - 1)
    def _():
        for g in range(SSM_GROUPS):
            hout_ref[0, g * gw:(g + 1) * gw, :] = ht_ref[g].T


def z_pad_slice(z_ref, ch):
    return _pad_rows(z_ref[0, :, ch], CHUNK)


def _ssd_call(z3, xbc3, att3, conv_prev, h0, layer, params, valid_last, out_dtype, h0_layered):
    nb, t, _ = z3.shape
    rows = min(t, CHUNK)
    n_chunks = t // rows
    conv_w, conv_b, dt_bias, a_log, d_skip, norm_g = params
    hp = SSM_HEADS * SSM_HEAD_DIM
    if h0_layered:
        cprev_spec = pl.BlockSpec((1, None, CONV_W - 1, XBC_DIM), lambda b, c: (b, layer, 0, 0))
        h0_spec = pl.BlockSpec((1, None, hp, SSM_STATE), lambda b, c: (b, layer, 0, 0))
    else:
        cprev_spec = pl.BlockSpec((1, CONV_W - 1, XBC_DIM), lambda b, c: (b, 0, 0))
        h0_spec = pl.BlockSpec((1, hp, SSM_STATE), lambda b, c: (b, 0, 0))
    lvec = lambda w: pl.BlockSpec((1, 1, w), lambda b, c: (layer, 0, 0))
    return pl.pallas_call(
        functools.partial(_ssd_body, valid_last=valid_last, n_chunks=n_chunks),
        grid=(nb, n_chunks),
        in_specs=[pl.BlockSpec((1, rows, SSM_DIM), lambda b, c: (b, c, 0)),
                  pl.BlockSpec((1, rows, XBC_DIM), lambda b, c: (b, c, 0)),
                  pl.BlockSpec((1, rows, LANES), lambda b, c: (b, c, ATT_DTWI_BLK)),
                  cprev_spec, h0_spec,
                  pl.BlockSpec((1, 8, XBC_DIM), lambda b, c: (layer, 0, 0)),
                  lvec(XBC_DIM), lvec(LANES), lvec(LANES), lvec(SSM_DIM), lvec(SSM_DIM)],
        out_specs=[pl.BlockSpec((1, rows, SSM_DIM), lambda b, c: (b, c, 0)),
                   pl.BlockSpec((1, hp, SSM_STATE), lambda b, c: (b, 0, 0)),
                   pl.BlockSpec((1, CONV_W - 1, XBC_DIM), lambda b, c: (b, 0, 0))],
        out_shape=[jax.ShapeDtypeStruct((nb, t, SSM_DIM), out_dtype),
                   jax.ShapeDtypeStruct((nb, hp, SSM_STATE), F32),
                   jax.ShapeDtypeStruct((nb, CONV_W - 1, XBC_DIM), F32)],
        scratch_shapes=[pltpu.VMEM((8 + CHUNK, XBC_DIM), F32),
                        pltpu.VMEM((SSM_GROUPS, SSM_STATE, hp // SSM_GROUPS), F32)],
        compiler_params=_cp(2),
    )(z3, xbc3, att3, conv_prev, h0, conv_w, conv_b, dt_bias, a_log, d_skip, norm_g)


def _gelu_tanh(x):
    return 0.5 * x * (1.0 + jnp.tanh(math.sqrt(2.0 / math.pi) * (x + 0.044715 * (x * x * x))))


def _gmlp_body(gm_ref, lg_ref, lb_ref, ws_ref, bs_ref, o_ref, v_ref):
    rows = gm_ref.shape[1]
    g = _gelu_tanh(gm_ref[0])
    u = g[:, :GM_DIM]
    v = g[:, GM_DIM:]
    mu = jnp.mean(v, axis=-1, keepdims=True)
    var = jnp.mean(jnp.square(v - mu), axis=-1, keepdims=True)
    v = (v - mu) * lax.rsqrt(var + LN_EPS) * lg_ref[0] + lb_ref[0]
    v_ref[0] = v
    vb = _pad_rows(v, CHUNK).astype(BF16)
    causal = (lax.broadcasted_iota(I32, (CHUNK, CHUNK), 1) <= lax.broadcasted_iota(I32, (CHUNK, CHUNK), 0))
    gd = GM_DIM // GM_GROUPS
    bs = bs_ref[0]
    for grp in range(GM_GROUPS):
        sl = slice(grp * gd, (grp + 1) * gd)
        wm = jnp.where(causal, ws_ref[0, grp], 0.0).astype(BF16)
        mix = _dot(wm, vb[:, sl]) + bs[:, grp:grp + 1]
        o_ref[0, :, sl] = (u[:, sl] * mix[:rows]).astype(o_ref.dtype)


def _gmlp_call(gm3, layer, params, out_dtype):
    nb, t, _ = gm3.shape
    rows = min(t, CHUNK)
    ln_g, ln_b, ws, bs_t = params
    lvec = lambda w: pl.BlockSpec((1, 1, w), lambda b, c: (layer, 0, 0))
    return pl.pallas_call(
        _gmlp_body,
        grid=(nb, t // rows),
        in_specs=[pl.BlockSpec((1, rows, 2 * GM_DIM), lambda b, c: (b, c, 0)),
                  lvec(GM_DIM), lvec(GM_DIM),
                  pl.BlockSpec((1, GM_GROUPS, CHUNK, CHUNK), lambda b, c: (layer, 0, 0, 0)),
                  pl.BlockSpec((1, CHUNK, LANES), lambda b, c: (layer, 0, 0))],
        out_specs=[pl.BlockSpec((1, rows, GM_DIM), lambda b, c: (b, c, 0)),
                   pl.BlockSpec((1, rows, GM_DIM), lambda b, c: (b, c, 0))],
        out_shape=[jax.ShapeDtypeStruct((nb, t, GM_DIM), out_dtype),
                   jax.ShapeDtypeStruct((nb, t, GM_DIM), F32)],
        compiler_params=_cp(2),
    )(gm3, ln_g, ln_b, ws, bs_t)


def _merge_body(h_ref, oa_ref, ob_ref, oc_ref, wga_ref, wgb_ref, wgc_ref, wba_ref, wbb_ref, wbc_ref, m_ref):
    h = h_ref[...]
    acc = None
    for o_ref, wg_ref, wb_ref in ((oa_ref, wga_ref, wba_ref), (ob_ref, wgb_ref, wbb_ref), (oc_ref, wgc_ref, wbc_ref)):
        term = jax.nn.sigmoid(_dot(h, wg_ref[0])) * _dot(o_ref[...], wb_ref[0, 0])
        acc = term if acc is None else acc + term
    m_ref[...] = acc.astype(m_ref.dtype)


def _merge_call(h, oa, ob, oc, w_gate, w_branch, layer, tm, tn):
    n, d = h.shape
    nj = d // tn
    row = lambda w: pl.BlockSpec((tm, w), lambda i, j: (i, 0))
    wg = lambda br: pl.BlockSpec((1, d, tn), lambda i, j: (layer, 0, br * nj + j))
    wb = lambda br: pl.BlockSpec((1, 1, BRANCH_DIM, tn), lambda i, j: (layer, br, 0, j))
    return pl.pallas_call(
        _merge_body,
        grid=(n // tm, nj),
        in_specs=[row(d), row(BRANCH_DIM), row(BRANCH_DIM), row(BRANCH_DIM),
                  wg(0), wg(1), wg(2), wb(0), wb(1), wb(2)],
        out_specs=pl.BlockSpec((tm, tn), lambda i, j: (i, j)),
        out_shape=jax.ShapeDtypeStruct((n, d), BF16),
        compiler_params=_cp(2),
    )(h, oa, ob, oc, w_gate, w_gate, w_gate, w_branch, w_branch, w_branch)


def _ln_finish(acc_ref, nj, tn, lg_ref, lb_ref, sc_ref, sh_ref, xo_ref, ho_ref, po_ref):
    tm = acc_ref.shape[1]
    d = nj * tn
    s = jnp.zeros((tm, 1), F32)
    for jj in range(nj):
        s = s + jnp.sum(acc_ref[jj], axis=1, keepdims=True)
    mu = s / d
    v = jnp.zeros((tm, 1), F32)
    for jj in range(nj):
        dlt = acc_ref[jj] - mu
        v = v + jnp.sum(dlt * dlt, axis=1, keepdims=True)
    rstd = lax.rsqrt(v / d + LN_EPS)

    def finish(jj):
        sl = slice(jj * tn, (jj + 1) * tn)
        y = (acc_ref[jj] - mu) * rstd * lg_ref[0, :, sl] + lb_ref[0, :, sl]
        xo_ref[:, sl] = y
        hmod = y * (1.0 + sc_ref[0, :, sl]) + sh_ref[0, :, sl]
        hb = hmod.astype(BF16)
        ho_ref[:, sl] = hb
        return hb

    for jj in range(nj // 2):
        lo = finish(jj)
        hi = finish(jj + nj // 2)
        if po_ref is not None:
            _pack_store(po_ref, lo, hi, jj * (tn // LANES), tm, d // 2 // LANES)


def _pack_store(po_ref, lo, hi, group0, tm, groups):
    lo_bits = lax.shift_right_logical(lax.bitcast_convert_type(lo.astype(F32), I32), jnp.int32(16))
    hi_bits = lax.bitcast_convert_type(hi.astype(F32), I32) & jnp.int32(-65536)
    word = lo_bits | hi_bits
    for g in range(word.shape[1] // LANES):
        po_ref[pl.ds(group0 + g, tm, stride=groups), :] = word[:, g * LANES:(g + 1) * LANES]


def _projln_body(a_ref, w_ref, x_ref, g_ref, lg_ref, lb_ref, sc_ref, sh_ref, xo_ref, ho_ref, po_ref, acc_ref,
                 *, nj, tn, alpha):
    j = pl.program_id(1)
    acc_ref[j] = _dot(a_ref[...], w_ref[0])

    @pl.when(j == nj - 1)
    def _():
        for jj in range(nj):
            sl = slice(jj * tn, (jj + 1) * tn)
            acc_ref[jj] = alpha * x_ref[:, sl] + g_ref[0, :, sl] * acc_ref[jj]
        _ln_finish(acc_ref, nj, tn, lg_ref, lb_ref, sc_ref, sh_ref, xo_ref, ho_ref, po_ref)


def _projln_call(grp, a, w_out, x, gate, ln_g, ln_b, sc, sh, layer, alpha, tn):
    n, d = x.shape
    tm = grp.tm
    nj = d // tn
    row = lambda w: pl.BlockSpec((tm, w), lambda i, j: (i, 0))
    lvec = pl.BlockSpec((1, 1, d), lambda i, j: (layer, 0, 0))
    return pl.pallas_call(
        functools.partial(_projln_body, nj=nj, tn=tn, alpha=alpha),
        grid=(n // tm, nj),
        in_specs=[row(d), pl.BlockSpec((1, d, tn), lambda i, j: (layer, 0, j)), row(d),
                  grp.mod_spec(), lvec, lvec, grp.mod_spec(), grp.mod_spec()],
        out_specs=[row(d), row(d), pl.BlockSpec((tm * (d // 2 // LANES), LANES), lambda i, j: (i, 0))],
        out_shape=[jax.ShapeDtypeStruct((n, d), F32), jax.ShapeDtypeStruct((n, d), BF16),
                   jax.ShapeDtypeStruct((n * (d // 2 // LANES), LANES), I32)],
        scratch_shapes=[pltpu.VMEM((nj, tm, tn), F32)],
        compiler_params=_cp(2),
    )(a, w_out, x, gate, ln_g, ln_b, sc, sh)


def _combineln_body(y_ref, tw_ref, x_ref, g_ref, lg_ref, lb_ref, sc_ref, sh_ref, xo_ref, ho_ref, acc_ref,
                    *, nj, tn, alpha):
    tw = tw_ref[...]
    for jj in range(nj):
        sl = slice(jj * tn, (jj + 1) * tn)
        moe = y_ref[0, :, sl] * tw[:, 0:1]
        for k in range(1, TOP_K):
            moe = moe + y_ref[k, :, sl] * tw[:, k:k + 1]
        acc_ref[jj] = alpha * x_ref[:, sl] + g_ref[0, :, sl] * moe
    _ln_finish(acc_ref, nj, tn, lg_ref, lb_ref, sc_ref, sh_ref, xo_ref, ho_ref, None)


def _combineln_call(grp, y4, row0, tw, x, gate, ln_g, ln_b, sc, sh, layer, alpha, tm):
    n, d = x.shape
    tn = 512
    nj = d // tn
    rb = row0 // tm
    g2 = _Group(grp.n, tm, grp.rows_per_seq, grp.per_row)
    row = lambda w: pl.BlockSpec((tm, w), lambda i: (i, 0))
    lvec = pl.BlockSpec((1, 1, d), lambda i: (layer, 0, 0))
    return pl.pallas_call(
        functools.partial(_combineln_body, nj=nj, tn=tn, alpha=alpha),
        grid=(n // tm,),
        in_specs=[pl.BlockSpec((TOP_K, tm, d), lambda i: (0, rb + i, 0)),
                  pl.BlockSpec((tm, LANES), lambda i: (rb + i, 0)), row(d),
                  g2.mod_spec(), lvec, lvec, g2.mod_spec(), g2.mod_spec()],
        out_specs=[row(d), row(d)],
        out_shape=[jax.ShapeDtypeStruct((n, d), F32), jax.ShapeDtypeStruct((n, d), BF16)],
        scratch_shapes=[pltpu.VMEM((nj, tm, tn), F32)],
        compiler_params=_cp(1),
    )(y4, tw, x, gate, ln_g, ln_b, sc, sh)


def _router_body(h_ref, w_ref, b_ref, ti_ref, tw_ref):
    lg = _dot(h_ref[...], w_ref[0]) + b_ref[0]
    tm = lg.shape[0]
    lane = lax.broadcasted_iota(I32, (tm, LANES), 1)
    lg = jnp.where(lane < N_EXPERTS, lg, -jnp.inf)
    vals, idxs = [], []
    for _ in range(TOP_K):
        m = jnp.max(lg, axis=1, keepdims=True)
        idx = jnp.min(jnp.where(lg == m, lane, LANES), axis=1, keepdims=True)
        vals.append(m)
        idxs.append(idx)
        lg = jnp.where(lane == idx, -jnp.inf, lg)
    es = [jnp.exp(v - vals[0]) for v in vals]
    tot = es[0] + es[1] + es[2] + es[3]
    ti = jnp.zeros((tm, LANES), I32)
    tw = jnp.zeros((tm, LANES), F32)
    for k in range(TOP_K):
        ti = jnp.where(lane == k, idxs[k], ti)
        tw = jnp.where(lane == k, es[k] / tot, tw)
    ti_ref[...] = ti
    tw_ref[...] = tw


def _router_call(h, w_router, b_router, layer, tm):
    n, d = h.shape
    return pl.pallas_call(
        _router_body,
        grid=(n // tm,),
        in_specs=[pl.BlockSpec((tm, d), lambda i: (i, 0)),
                  pl.BlockSpec((1, d, LANES), lambda i: (layer, 0, 0)),
                  pl.BlockSpec((1, 1, LANES), lambda i: (layer, 0, 0))],
        out_specs=[pl.BlockSpec((tm, LANES), lambda i: (i, 0)), pl.BlockSpec((tm, LANES), lambda i: (i, 0))],
        out_shape=[jax.ShapeDtypeStruct((n, LANES), I32), jax.ShapeDtypeStruct((n, LANES), F32)],
        compiler_params=_cp(1),
    )(h, w_router, b_router)


def _rank_body(ti_ref, rank_ref, cnt_ref, carry_ref):
    i = pl.program_id(0)

    @pl.when(i == 0)
    def _():
        carry_ref[...] = jnp.zeros_like(carry_ref)

    ti = ti_ref[...]
    tm = ti.shape[0]
    lane = lax.broadcasted_iota(I32, (tm, LANES), 1)
    onehot = jnp.zeros((tm, LANES), F32)
    for k in range(TOP_K):
        onehot = onehot + (lane == ti[:, k:k + 1]).astype(F32)
    strict = (lax.broadcasted_iota(I32, (tm, tm), 1) < lax.broadcasted_iota(I32, (tm, tm), 0)).astype(BF16)
    before = _dot(strict, onehot.astype(BF16)) + carry_ref[0:1, :]
    rank = jnp.zeros((tm, LANES), F32)
    for k in range(TOP_K):
        rk = jnp.sum(jnp.where(lane == ti[:, k:k + 1], before, 0.0), axis=1, keepdims=True)
        rank = jnp.where(lane == k, rk, rank)
    rank_ref[...] = rank.astype(I32)
    total = carry_ref[0:1, :] + jnp.sum(onehot, axis=0, keepdims=True)
    carry_ref[...] = jnp.broadcast_to(total, carry_ref.shape)
    cnt_ref[...] = jnp.broadcast_to(total, cnt_ref.shape).astype(I32)


def _rank_call(ti, tm):
    n = ti.shape[0]
    return pl.pallas_call(
        _rank_body,
        grid=(n // tm,),
        in_specs=[pl.BlockSpec((tm, LANES), lambda i: (i, 0))],
        out_specs=[pl.BlockSpec((tm, LANES), lambda i: (i, 0)), pl.BlockSpec((8, LANES), lambda i: (0, 0))],
        out_shape=[jax.ShapeDtypeStruct((n, LANES), I32), jax.ShapeDtypeStruct((8, LANES), I32)],
        scratch_shapes=[pltpu.VMEM((8, LANES), F32)],
        compiler_params=_cp(1),
    )(ti)


def _dispatch_body(dst_idx_ref, hp_ref, xs_init_ref, xs_ref, sem, *, tokens, rows):
    del xs_init_ref

    def slot_copy(i, d):
        t = lax.shift_right_logical(i, 2)
        return pltpu.make_async_copy(hp_ref.at[pl.ds(pl.multiple_of(t * rows, rows), rows)],
                                     xs_ref.at[pl.ds(pl.multiple_of(d * rows, rows), rows)], sem)

    def issue(i, carry):
        slot_copy(i, dst_idx_ref[i]).start()
        return carry

    lax.fori_loop(0, tokens * TOP_K, issue, 0, unroll=8)

    def wait(i, carry):
        slot_copy(i, 0).wait()
        return carry

    lax.fori_loop(0, tokens * TOP_K, wait, 0, unroll=8)


def _dispatch_call(dst_idx, hp, cap, rows, tokens):
    assert TOP_K == 4
    n = hp.shape[0] // rows
    shape = (cap * rows, LANES)
    any_spec = pl.BlockSpec(memory_space=pl.ANY)
    return pl.pallas_call(
        functools.partial(_dispatch_body, tokens=tokens, rows=rows),
        grid=(n // tokens,),
        in_specs=[pl.BlockSpec((tokens * TOP_K,), lambda i: (i,), memory_space=pltpu.SMEM),
                  pl.BlockSpec((tokens * rows, LANES), lambda i: (i, 0)), any_spec],
        out_specs=any_spec,
        out_shape=jax.ShapeDtypeStruct(shape, hp.dtype),
        scratch_shapes=[pltpu.SemaphoreType.DMA(())],
        input_output_aliases={2: 0},
        compiler_params=pltpu.CompilerParams(dimension_semantics=("arbitrary",), has_side_effects=True,
                                             disable_bounds_checks=True),
    )(dst_idx, hp, jnp.zeros(shape, hp.dtype))


def _collect_body(src_idx_ref, ys_ref, y4_ref, sem, *, tokens):
    def slot_copy(i, r):
        t = lax.shift_right_logical(i, 2)
        k = i & 3
        return pltpu.make_async_copy(ys_ref.at[pl.ds(r, 1)], y4_ref.at[k, pl.ds(t, 1)], sem)

    def issue(i, carry):
        slot_copy(i, src_idx_ref[i]).start()
        return carry

    lax.fori_loop(0, tokens * TOP_K, issue, 0, unroll=8)

    def wait(i, carry):
        slot_copy(i, 0).wait()
        return carry

    lax.fori_loop(0, tokens * TOP_K, wait, 0, unroll=8)


def _collect_call(src_idx, ys, n, tokens):
    assert TOP_K == 4
    d = ys.shape[1]
    return pl.pallas_call(
        functools.partial(_collect_body, tokens=tokens),
        grid=(n // tokens,),
        in_specs=[pl.BlockSpec((tokens * TOP_K,), lambda i: (i,), memory_space=pltpu.SMEM),
                  pl.BlockSpec(memory_space=pl.ANY)],
        out_specs=pl.BlockSpec((TOP_K, tokens, d), lambda i: (0, i, 0)),
        out_shape=jax.ShapeDtypeStruct((TOP_K, n, d), ys.dtype),
        scratch_shapes=[pltpu.SemaphoreType.DMA(())],
        compiler_params=pltpu.CompilerParams(dimension_semantics=("arbitrary",), has_side_effects=True,
                                             disable_bounds_checks=True),
    )(src_idx, ys)


def _unpack(xp):
    lo = lax.bitcast_convert_type(lax.shift_left(xp, jnp.int32(16)), F32).astype(BF16)
    hi = lax.bitcast_convert_type(xp & jnp.int32(-65536), F32).astype(BF16)
    return lo, hi


def _expert1_body(blk_ref, e_ref, f_ref, wf_ref, first_ref, n_ref, xs_ref, wg_ref, wl_ref, bg_ref, bl_ref, act_ref,
                  wgb_ref, wlb_ref):
    s = pl.program_id(0)

    @pl.when(s < n_ref[0])
    def _():
        @pl.when(first_ref[s] == 1)
        def _():
            wgb_ref[...] = wg_ref[0, 0].astype(BF16)
            wlb_ref[...] = wl_ref[0, 0].astype(BF16)

        groups = xs_ref.shape[0] // MOE_TM
        words = jnp.concatenate([xs_ref[pl.ds(g, MOE_TM, stride=groups), :] for g in range(groups)], axis=1)
        lo, hi = _unpack(words)
        half = lo.shape[1]
        glu = _dot(lo, wgb_ref[0:half, :]) + _dot(hi, wgb_ref[half:, :]) + bg_ref[0, 0]
        lin = _dot(lo, wlb_ref[0:half, :]) + _dot(hi, wlb_ref[half:, :]) + bl_ref[0, 0]
        glu = jnp.minimum(glu, SWIGLU_LIMIT)
        lin = jnp.clip(lin, -SWIGLU_LIMIT, SWIGLU_LIMIT)
        act_ref[...] = (glu * jax.nn.sigmoid(SWIGLU_ALPHA * glu) * (lin + 1.0)).astype(BF16)

    @pl.when(s >= n_ref[0])
    def _():
        act_ref[...] = jnp.zeros_like(act_ref)


def _expert1_call(plan, xs, w1, b1, layer):
    d = w1.shape[2]
    groups = d // 2 // LANES
    cap = xs.shape[0] // groups
    n_steps_max = plan["step_blk"].shape[0]
    nf = EXPERT_DIM // MOE_TF
    grid_spec = pltpu.PrefetchScalarGridSpec(
        num_scalar_prefetch=6,
        grid=(n_steps_max,),
        in_specs=[pl.BlockSpec((MOE_TM * groups, LANES), lambda s, blk, e, f, wf, fi, n: (blk[s], 0)),
                  pl.BlockSpec((1, 1, d, MOE_TF), lambda s, blk, e, f, wf, fi, n: (layer, e[s], 0, wf[s])),
                  pl.BlockSpec((1, 1, d, MOE_TF), lambda s, blk, e, f, wf, fi, n: (layer, e[s], 0, nf + wf[s])),
                  pl.BlockSpec((1, 1, 1, MOE_TF), lambda s, blk, e, f, wf, fi, n: (layer, e[s], 0, wf[s])),
                  pl.BlockSpec((1, 1, 1, MOE_TF), lambda s, blk, e, f, wf, fi, n: (layer, e[s], 0, nf + wf[s]))],
        out_specs=pl.BlockSpec((MOE_TM, MOE_TF), lambda s, blk, e, f, wf, fi, n: (blk[s], f[s])),
        scratch_shapes=[pltpu.VMEM((d, MOE_TF), BF16), pltpu.VMEM((d, MOE_TF), BF16)],
    )
    return pl.pallas_call(
        _expert1_body,
        grid_spec=grid_spec,
        out_shape=jax.ShapeDtypeStruct((cap, EXPERT_DIM), BF16),
        compiler_params=_cp(1),
    )(plan["step_blk"], plan["step_e"], plan["step_f"], plan["step_wf"], plan["step_first"], plan["n_steps"],
      xs, w1, w1, b1, b1)


def _expert2_body(e_ref, first_ref, n_ref, act_ref, w_ref, b_ref, y_ref, wb_ref):
    i = pl.program_id(0)

    @pl.when(i < n_ref[0])
    def _():
        @pl.when(first_ref[i] == 1)
        def _():
            wb_ref[...] = w_ref[0, 0].astype(BF16)

        y_ref[...] = _dot(act_ref[...], wb_ref[...]) + b_ref[0, 0]

    @pl.when(i >= n_ref[0])
    def _():
        y_ref[...] = jnp.zeros_like(y_ref)


def _expert2_call(plan, act, w2, b2, layer):
    cap, f = act.shape
    d = w2.shape[-1]
    nblk = cap // MOE_TM
    grid_spec = pltpu.PrefetchScalarGridSpec(
        num_scalar_prefetch=3,
        grid=(nblk,),
        in_specs=[pl.BlockSpec((MOE_TM, f), lambda i, e, fi, n: (i, 0)),
                  pl.BlockSpec((1, 1, f, d), lambda i, e, fi, n: (layer, e[i], 0, 0)),
                  pl.BlockSpec((1, 1, 1, d), lambda i, e, fi, n: (layer, e[i], 0, 0))],
        out_specs=pl.BlockSpec((MOE_TM, d), lambda i, e, fi, n: (i, 0)),
        scratch_shapes=[pltpu.VMEM((f, d), BF16)],
    )
    return pl.pallas_call(
        _expert2_body,
        grid_spec=grid_spec,
        out_shape=jax.ShapeDtypeStruct((cap, d), F32),
        compiler_params=_cp(1),
    )(plan["blk_e"], plan["blk_first"], plan["n_blk"], act, w2, b2)


def _moe_plan(counts, top_i, rank, nblk_max):
    nb = (counts + MOE_TM - 1) // MOE_TM
    pad_end = jnp.cumsum(nb * MOE_TM)
    offsets = pad_end - nb * MOE_TM
    cumblk = jnp.cumsum(nb)
    blk_start = cumblk - nb
    n_blk = cumblk[-1]
    dest = jnp.take(offsets, top_i) + rank
    last = N_EXPERTS - 1
    bi = jnp.minimum(jnp.arange(nblk_max, dtype=I32), n_blk - 1)
    count_le = lambda ends, v: jnp.sum((ends[None, :] <= v[:, None]).astype(I32), axis=1)
    blk_e = jnp.minimum(count_le(cumblk, bi), last).astype(I32)
    blk_first = (bi == jnp.take(blk_start, blk_e)).astype(I32)
    nf = EXPERT_DIM // MOE_TF
    step_end = jnp.cumsum(nf * nb)
    n_steps = step_end[-1]
    s_all = jnp.arange(nf * nblk_max, dtype=I32)
    si = jnp.minimum(s_all, n_steps - 1)
    se = jnp.minimum(count_le(step_end, si), last).astype(I32)
    nbe = jnp.maximum(jnp.take(nb, se), 1)
    local = si - (jnp.take(step_end, se) - nf * jnp.take(nb, se))
    spare = s_all - n_steps
    swf = (local // nbe).astype(I32)
    sf = jnp.where(spare >= 0, spare % nf, swf).astype(I32)
    sblk = jnp.where(spare >= 0, n_blk + spare // nf, jnp.take(blk_start, se) + local % nbe).astype(I32)
    sfirst = (local % nbe == 0).astype(I32)
    return dict(dest=dest.astype(I32), blk_e=blk_e, blk_first=blk_first, n_blk=n_blk.reshape(1).astype(I32),
                step_blk=sblk, step_e=se, step_f=sf, step_wf=swf, step_first=sfirst,
                n_steps=n_steps.reshape(1).astype(I32))


def _moe(hp_all, ti_all, w1, b1, w2, b2, layer):
    n = ti_all.shape[0]
    d = w1.shape[2]
    rank_all, counts8 = _rank_call(ti_all, 64)
    counts = counts8[0, :N_EXPERTS]
    n_assign = n * TOP_K
    nblk_max = -(-(n_assign + N_EXPERTS * (MOE_TM - 1)) // MOE_TM)
    cap = nblk_max * MOE_TM
    plan = _moe_plan(counts, ti_all[:, :TOP_K], rank_all[:, :TOP_K], nblk_max)
    dest = plan["dest"].reshape(-1)
    tokens = 64
    xs = _dispatch_call(dest, hp_all, cap, d // 2 // LANES, tokens)
    act = _expert1_call(plan, xs, w1, b1, layer)
    ys = _expert2_call(plan, act, w2, b2, layer)
    return _collect_call(dest, ys, n, tokens)


def _mixers(grp, h, att, z, xbc, gm, nb, t, layer, prm, tables, tab_blocks, past, ssm_state, out_dtype):
    k_f, ki_f, kb, vb, kib = _kvpost_call(att, tables, tab_blocks, prm["idx_kn_g"], prm["idx_kn_b"], layer,
                                          min(grp.n, 512))
    att3 = att.reshape(nb, t, ATT_COLS)
    if past is None:
        nk = t
        q_pos0 = 0
        kb3, vb3, kib3 = (a.reshape(nb, t, -1) for a in (kb, vb, kib))
        tq = CHUNK
    else:
        page_table, cache_k, cache_v, cache_idx_k = past
        v_f = att[:, ATT_V_BLK * KV_DIM:(ATT_V_BLK + 1) * KV_DIM]
        kb3, vb3, kib3 = _pages_call(page_table, cache_k, cache_v, cache_idx_k, k_f.reshape(nb, t, KV_DIM),
                                     v_f.reshape(nb, t, KV_DIM), ki_f.reshape(nb, t, IDX_DIM), layer)
        q_pos0 = page_table.shape[1] * PAGE_SIZE
        nk = q_pos0 + prm["t_valid"]
        tq = t
    topk = min(TOPK_MAX, nk // 4)
    if past is None:
        n_qblk = t // tq
        per_call = 2 if n_qblk % 2 == 0 else 1
        o_a = jnp.concatenate(
            [_dsa_call(att3, tables, kb3, vb3, kib3, tq, q_pos0, topk, out_dtype, blk0=b0, n_blk=per_call,
                       nk=(b0 + per_call) * tq) for b0 in range(0, n_qblk, per_call)], axis=1)
    else:
        o_a = _dsa_call(att3, tables, kb3, vb3, kib3, tq, q_pos0, topk, out_dtype)
    ssm_params = (prm["conv_w"], prm["conv_b"], prm["dt_bias"], prm["a_log"], prm["d_skip"], prm["ssm_norm_g"])
    conv_prev, h0, layered = ssm_state
    valid_last = prm["t_valid"] if t < CHUNK else CHUNK
    o_b, ssm_new, conv_new = _ssd_call(z.reshape(nb, t, SSM_DIM), xbc.reshape(nb, t, XBC_DIM), att3, conv_prev, h0,
                                       layer, ssm_params, valid_last, out_dtype, layered)
    gm_params = (prm["gm_ln_g"], prm["gm_ln_b"], prm["gm_ws"], prm["gm_bs_t"])
    o_c, gm_v = _gmlp_call(gm.reshape(nb, t, 2 * GM_DIM), layer, gm_params, out_dtype)
    flat = lambda a: a.reshape(nb * t, -1).astype(BF16)
    return (flat(o_a), flat(o_b), flat(o_c)), (k_f, ki_f, ssm_new, conv_new, gm_v)


def kernel(x_prompt, x_sample, cache_k, cache_v, cache_idx_k, state_ssm, state_conv, page_table, c_prompt, c_sample, ln1_g, ln1_b, ln2_g, ln2_b, w_ada, b_ada, w_in, idx_kn_g, idx_kn_b, conv_w, conv_b, dt_bias, a_log, d_skip, ssm_norm_g, gm_ln_g, gm_ln_b, gm_ws, gm_bs, w_gate, w_branch, w_out, w_router, b_router, w1, b1, w2, b2):
    bp, sp, d = x_prompt.shape
    db, ds, _ = x_sample.shape
    depth = w_in.shape[0]
    alpha = (2 * depth) ** 0.25
    past_len = page_table.shape[1] * PAGE_SIZE
    n_p, n_s = bp * sp, db * SAMPLE_ROWS
    grp_p = _Group(n_p, 256, sp, False)
    grp_s = _Group(n_s, n_s, SAMPLE_ROWS, True)

    o = np.cumsum([0, ATT_DIM, KV_DIM, KV_DIM, IDX_HEADS * IDX_DIM, IDX_DIM, IDX_HEADS, SSM_DIM, XBC_DIM, SSM_HEADS,
                   2 * GM_DIM])
    seg = lambda i, j=None: w_in[:, :, o[i]:o[(i if j is None else j) + 1]].astype(BF16)
    pad_cols = LANES - SSM_HEADS - IDX_HEADS
    w_proj = jnp.concatenate([seg(6), seg(9), seg(7), seg(0, 4), seg(8), seg(5),
                              jnp.zeros((depth, d, pad_cols), BF16)], axis=-1)
    w_gate_b = w_gate.astype(BF16)
    w_branch_b = w_branch.astype(BF16)
    w_out_b = w_out.astype(BF16)
    w_router_b = jnp.pad(w_router, ((0, 0), (0, 0), (0, LANES - N_EXPERTS))).astype(BF16)
    lane_pad = lambda a: jnp.pad(a, ((0, 0), (0, LANES - a.shape[-1])))[:, None, :]
    vec = lambda a: a[:, None, :]
    prm = dict(
        idx_kn_g=vec(idx_kn_g), idx_kn_b=vec(idx_kn_b),
        conv_w=jnp.pad(conv_w, ((0, 0), (0, 8 - CONV_W), (0, 0))), conv_b=vec(conv_b),
        dt_bias=lane_pad(dt_bias), a_log=lane_pad(a_log),
        d_skip=vec(jnp.repeat(d_skip, SSM_HEAD_DIM, axis=-1)), ssm_norm_g=vec(ssm_norm_g),
        gm_ln_g=vec(gm_ln_g), gm_ln_b=vec(gm_ln_b), gm_ws=gm_ws,
        gm_bs_t=jnp.pad(jnp.swapaxes(gm_bs, 1, 2), ((0, 0), (0, 0), (0, LANES - GM_GROUPS))),
    )
    b_router_p = lane_pad(b_router)
    b1_4 = b1[:, :, None, :]
    b2_4 = b2[:, :, None, :]
    ln = dict(g1=vec(ln1_g), b1=vec(ln1_b), g2=vec(ln2_g), b2=vec(ln2_b))

    tab_p = _rope_tables(jnp.arange(sp, dtype=I32))
    pos_s = past_len + jnp.arange(SAMPLE_ROWS, dtype=I32)
    tab_s1 = _rope_tables(pos_s)
    tab_s = tuple(jnp.tile(a, (db, 1)) for a in tab_s1)

    n_c = bp + db
    c_rows = jnp.concatenate([c_prompt, c_sample, jnp.zeros((-n_c % 8, d), F32)], axis=0)
    mod = _mod_call(c_rows, w_ada, b_ada).reshape(depth, c_rows.shape[0], 6, d)

    def mods(layer, which, grp, lo, hi):
        return grp.mod_array(mod[layer, lo:hi, which])

    zero_mod_p = grp_p.mod_array(jnp.zeros((bp, d), F32))
    zero_mod_s = grp_s.mod_array(jnp.zeros((db, d), F32))

    xs_pad = jnp.pad(x_sample, ((0, 0), (0, SAMPLE_ROWS - ds), (0, 0))).reshape(n_s, d)
    xp = x_prompt.reshape(n_p, d)
    xs_ = xs_pad
    hp = _modulate_call(grp_p, xp, mods(0, 1, grp_p, 0, bp), mods(0, 0, grp_p, 0, bp))
    hs = _modulate_call(grp_s, xs_, mods(0, 1, grp_s, bp, n_c), mods(0, 0, grp_s, bp, n_c))

    conv0 = jnp.zeros((bp, CONV_W - 1, XBC_DIM), F32)
    ssm0 = jnp.zeros((bp, SSM_HEADS * SSM_HEAD_DIM, SSM_STATE), F32)
    state_ssm4 = state_ssm.reshape(db, depth, SSM_HEADS * SSM_HEAD_DIM, SSM_STATE)

    outs_p = [[] for _ in range(5)]
    outs_s = [[] for _ in range(6)]
    for layer in range(depth):
        groups = (
            (grp_p, hp, xp, bp, sp, tab_p, sp // min(n_p, 512), None, (conv0, ssm0, False), BF16, min(n_p, 1024),
             0, bp),
            (grp_s, hs, xs_, db, SAMPLE_ROWS, tab_s, 1, (page_table, cache_k, cache_v, cache_idx_k),
             (state_conv, state_ssm4, True), F32, n_s, bp, n_c),
        )
        x1s, h2s, packs, tis, tws = [], [], [], [], []
        for grp, h, x, nb, t, tabs, tab_blocks, past, ssm_state, odt, tm_mm, lo, hi in groups:
            z = _mm_call(h, w_proj, layer, COL_Z, SSM_DIM, tm_mm, 512)
            gm = _mm_call(h, w_proj, layer, COL_GM, 2 * GM_DIM, tm_mm, 512)
            xbc = _mm_call(h, w_proj, layer, COL_XBC, XBC_DIM, tm_mm, 512)
            att = _mm_call(h, w_proj, layer, COL_ATT, ATT_COLS, tm_mm, 256)
            prm_g = dict(prm, t_valid=(t if past is None else ds))
            branches, new = _mixers(grp, h, att, z, xbc, gm, nb, t, layer, prm_g, tabs, tab_blocks, past, ssm_state, odt)
            k_f, ki_f, ssm_new, conv_new, gm_v = new
            v_f = att[:, ATT_V_BLK * KV_DIM:(ATT_V_BLK + 1) * KV_DIM]
            if past is None:
                outs_p[0].append(k_f.reshape(nb, t, N_KV_HEADS, HEAD_DIM))
                outs_p[1].append(v_f.reshape(nb, t, N_KV_HEADS, HEAD_DIM))
                outs_p[2].append(ki_f.reshape(nb, t, IDX_DIM))
                outs_p[3].append(ssm_new.reshape(nb, SSM_HEADS, SSM_HEAD_DIM, SSM_STATE))
                outs_p[4].append(conv_new)
            else:
                outs_s[0].append(k_f.reshape(nb, t, N_KV_HEADS, HEAD_DIM)[:, :ds])
                outs_s[1].append(v_f.reshape(nb, t, N_KV_HEADS, HEAD_DIM)[:, :ds])
                outs_s[2].append(ki_f.reshape(nb, t, IDX_DIM)[:, :ds])
                outs_s[3].append(ssm_new.reshape(nb, SSM_HEADS, SSM_HEAD_DIM, SSM_STATE))
                outs_s[4].append(conv_new)
                outs_s[5].append(gm_v[:, :ds])
            merged = _merge_call(h, *branches, w_gate_b, w_branch_b, layer, min(grp.n, 512), 256)
            x1, h2, pack = _projln_call(grp, merged, w_out_b, x, mods(layer, 2, grp, lo, hi), ln["g1"], ln["b1"],
                                        mods(layer, 4, grp, lo, hi), mods(layer, 3, grp, lo, hi), layer, alpha, 512)
            ti, tw = _router_call(h2, w_router_b, b_router_p, layer, min(grp.n, 512))
            x1s.append(x1)
            packs.append(pack)
            tis.append(ti)
            tws.append(tw)
        hp_all = jnp.concatenate(packs, axis=0)
        ti_all = jnp.concatenate(tis, axis=0)
        tw_all = jnp.concatenate(tws, axis=0)
        y4 = _moe(hp_all, ti_all, w1, b1_4, w2, b2_4, layer)
        nxt = layer + 1
        new_x, new_h = [], []
        for (grp, lo, hi, zero_mod, row0, tm), x1 in zip(
                ((grp_p, 0, bp, zero_mod_p, 0, 128), (grp_s, bp, n_c, zero_mod_s, n_p, n_s)), x1s):
            sc = mods(nxt, 1, grp, lo, hi) if nxt < depth else zero_mod
            sh = mods(nxt, 0, grp, lo, hi) if nxt < depth else zero_mod
            xo, ho = _combineln_call(grp, y4, row0, tw_all, x1, mods(layer, 5, grp, lo, hi), ln["g2"], ln["b2"],
                                     sc, sh, layer, alpha, tm)
            new_x.append(xo)
            new_h.append(ho)
        (xp, xs_), (hp, hs) = new_x, new_h

    stack = lambda lst: jnp.stack(lst, axis=1)
    y_prompt = xp.reshape(bp, sp, d)
    y_sample = xs_.reshape(db, SAMPLE_ROWS, d)[:, :ds]
    return (y_prompt, y_sample, *(stack(a) for a in outs_p), *(stack(a) for a in outs_s))
```

```python
import functools
import math

import jax
import jax.numpy as jnp
import numpy as np
from jax import lax
from jax.experimental import pallas as pl
from jax.experimental.pallas import tpu as pltpu

F32 = jnp.float32
BF16 = jnp.bfloat16
I32 = jnp.int32

D_MODEL = 4096
PAGE_SIZE = 128
N_HEADS = 16
HEAD_DIM = 128
N_KV_HEADS = 4
Q_PER_KV = N_HEADS // N_KV_HEADS
ATT_DIM = N_HEADS * HEAD_DIM
KV_DIM = N_KV_HEADS * HEAD_DIM
ROT_DIM = HEAD_DIM // 4
IDX_HEADS = 8
IDX_DIM = 128
TOPK_MAX = 256
ROPE_THETA = 500000.0
ATT_SCALE = HEAD_DIM ** -0.5
IDX_SCALE = (IDX_DIM * IDX_HEADS) ** -0.5
SSM_DIM = 2048
SSM_HEAD_DIM = 64
SSM_HEADS = SSM_DIM // SSM_HEAD_DIM
SSM_GROUPS = 4
SSM_STATE = 128
CONV_W = 4
XBC_DIM = SSM_DIM + 2 * SSM_GROUPS * SSM_STATE
GM_DIM = 2048
GM_GROUPS = 16
N_BRANCH = 3
BRANCH_DIM = 2048
N_EXPERTS = 32
TOP_K = 4
EXPERT_DIM = 1024
SWIGLU_ALPHA = 1.702
SWIGLU_LIMIT = 7.0
LN_EPS = 1e-5
RMS_EPS = 1e-5

CHUNK = 128
SAMPLE_ROWS = 8
LANES = 128
ATT_COLS = ATT_DIM + 2 * KV_DIM + IDX_HEADS * IDX_DIM + IDX_DIM + LANES
ATT_K_BLK = ATT_DIM // KV_DIM
ATT_V_BLK = ATT_K_BLK + 1
ATT_QI_BLK = (ATT_DIM + 2 * KV_DIM) // (IDX_HEADS * IDX_DIM)
ATT_KI_BLK = (ATT_DIM + 2 * KV_DIM + IDX_HEADS * IDX_DIM) // LANES
ATT_DTWI_BLK = ATT_KI_BLK + 1
WI_LANE = SSM_HEADS
MOE_TM = 256
MOE_TF = 512
VMEM_LIMIT = 56 * 1024 * 1024


def _cp(n_axes, vmem=VMEM_LIMIT):
    return pltpu.CompilerParams(dimension_semantics=("arbitrary",) * n_axes, vmem_limit_bytes=vmem)


def _dot(a, b):
    return jnp.dot(a, b, preferred_element_type=F32)


def _dot_nt(a, b):
    return lax.dot_general(a, b, (((1,), (1,)), ((), ())), preferred_element_type=F32)


def _split3(x):
    hi = x.astype(BF16)
    r1 = x - hi.astype(F32)
    mid = r1.astype(BF16)
    lo = (r1 - mid.astype(F32)).astype(BF16)
    return hi, mid, lo


def _dot3_left(m_bf16, x):
    hi, mid, lo = _split3(x)
    return _dot(m_bf16, hi) + _dot(m_bf16, mid) + _dot(m_bf16, lo)


def _dot3_right(x, m_bf16):
    hi, mid, lo = _split3(x)
    return _dot(hi, m_bf16) + _dot(mid, m_bf16) + _dot(lo, m_bf16)


def _silu(x):
    return x * jax.nn.sigmoid(x)


def _rope(x, c, s1, s2):
    half = ROT_DIM // 2
    return x * c + pltpu.roll(x, LANES - half, 1) * s1 + pltpu.roll(x, half, 1) * s2


def _rope_tables(pos):
    half = ROT_DIM // 2
    inv = ROPE_THETA ** (-jnp.arange(half, dtype=F32) / half)
    ang = pos.astype(F32)[:, None] * inv
    cos, sin = jnp.cos(ang), jnp.sin(ang)
    n = pos.shape[0]
    rest = LANES - ROT_DIM
    c = jnp.concatenate([cos, cos, jnp.ones((n, rest), F32)], axis=1)
    s1 = jnp.concatenate([-sin, jnp.zeros((n, LANES - half), F32)], axis=1)
    s2 = jnp.concatenate([jnp.zeros((n, half), F32), sin, jnp.zeros((n, rest), F32)], axis=1)
    return c, s1, s2


def _mod_body(c_ref, w_ref, b_ref, o_ref):
    s = _silu(c_ref[...]).astype(BF16)
    o_ref[0] = _dot(s, w_ref[0].astype(BF16)) + b_ref[0]


def _mod_call(c_rows, w_ada, b_ada):
    n_layers, d, n = w_ada.shape
    r = c_rows.shape[0]
    tn = 512
    return pl.pallas_call(
        _mod_body,
        grid=(n_layers, n // tn),
        in_specs=[pl.BlockSpec((r, d), lambda l, j: (0, 0)),
                  pl.BlockSpec((1, d, tn), lambda l, j: (l, 0, j)),
                  pl.BlockSpec((1, 1, tn), lambda l, j: (l, 0, j))],
        out_specs=pl.BlockSpec((1, r, tn), lambda l, j: (l, 0, j)),
        out_shape=jax.ShapeDtypeStruct((n_layers, r, n), F32),
        compiler_params=_cp(2),
    )(c_rows, w_ada, b_ada.reshape(n_layers, 1, n))


class _Group:
    def __init__(self, n, tm, rows_per_seq, per_row):
        self.n, self.tm, self.rows_per_seq, self.per_row = n, tm, rows_per_seq, per_row

    def mod_array(self, m):
        if self.per_row:
            return jnp.repeat(m, self.rows_per_seq, axis=0)[None]
        return m[:, None, :]

    def mod_spec(self, width=D_MODEL):
        if self.per_row:
            return pl.BlockSpec((1, self.n, width), lambda i, *_: (0, 0, 0))
        tm, rps = self.tm, self.rows_per_seq
        return pl.BlockSpec((1, 1, width), lambda i, *_: ((i * tm) // rps, 0, 0))


def _modulate_body(x_ref, sc_ref, sh_ref, h_ref):
    h_ref[...] = (x_ref[...] * (1.0 + sc_ref[0]) + sh_ref[0]).astype(BF16)


def _modulate_call(grp, x, sc, sh):
    n, d = x.shape
    tm = grp.tm
    return pl.pallas_call(
        _modulate_body,
        grid=(n // tm,),
        in_specs=[pl.BlockSpec((tm, d), lambda i: (i, 0)), grp.mod_spec(), grp.mod_spec()],
        out_specs=pl.BlockSpec((tm, d), lambda i: (i, 0)),
        out_shape=jax.ShapeDtypeStruct((n, d), BF16),
        compiler_params=_cp(1),
    )(x, sc, sh)


def _mm_body(a_ref, w_ref, o_ref):
    o_ref[...] = _dot(a_ref[...], w_ref[0]).astype(o_ref.dtype)


def _mm_call(a, w, layer, col0, ncols, tm, tn, out_dtype=F32):
    n, k = a.shape
    cb = col0 // tn
    return pl.pallas_call(
        _mm_body,
        grid=(n // tm, ncols // tn),
        in_specs=[pl.BlockSpec((tm, k), lambda i, j: (i, 0)),
                  pl.BlockSpec((1, k, tn), lambda i, j: (layer, 0, cb + j))],
        out_specs=pl.BlockSpec((tm, tn), lambda i, j: (i, j)),
        out_shape=jax.ShapeDtypeStruct((n, ncols), out_dtype),
        compiler_params=_cp(2),
    )(a, w)


def _segment_body(a_ref, b_ref, o_ref, *, shift):
    if shift == 0:
        o_ref[0] = a_ref[0].astype(BF16)
        return
    ra = pltpu.roll(a_ref[0], LANES - shift, 1)
    rb = pltpu.roll(b_ref[0], LANES - shift, 1)
    lane = lax.broadcasted_iota(I32, ra.shape, 1)
    o_ref[0] = jnp.where(lane < LANES - shift, ra, rb).astype(BF16)


def _segment_call(w, col0, width):
    n_layers, k, n = w.shape
    b0, shift = divmod(int(col0), LANES)
    last = -(-n // LANES) - 1
    return pl.pallas_call(
        functools.partial(_segment_body, shift=shift),
        grid=(n_layers, width // LANES),
        in_specs=[pl.BlockSpec((1, k, LANES), lambda l, j: (l, 0, b0 + j)),
                  pl.BlockSpec((1, k, LANES), lambda l, j: (l, 0, jnp.minimum(b0 + j + 1, last)))],
        out_specs=pl.BlockSpec((1, k, LANES), lambda l, j: (l, 0, j)),
        out_shape=jax.ShapeDtypeStruct((n_layers, k, width), BF16),
        compiler_params=_cp(2),
    )(w, w)


def _kvpost_body(k_ref, v_ref, ki_ref, c_ref, s1_ref, s2_ref, g_ref, b_ref,
                 ko_ref, kio_ref, kb_ref, vb_ref, kib_ref):
    c, s1, s2 = c_ref[...], s1_ref[...], s2_ref[...]
    for h in range(N_KV_HEADS):
        sl = slice(h * HEAD_DIM, (h + 1) * HEAD_DIM)
        kh = _rope(k_ref[:, sl], c, s1, s2)
        ko_ref[:, sl] = kh
        kb_ref[:, sl] = kh.astype(BF16)
    vb_ref[...] = v_ref[...].astype(BF16)
    ki = ki_ref[...]
    mu = jnp.mean(ki, axis=-1, keepdims=True)
    var = jnp.mean(jnp.square(ki - mu), axis=-1, keepdims=True)
    kin = (ki - mu) * lax.rsqrt(var + LN_EPS) * g_ref[0] + b_ref[0]
    kir = _rope(kin, c, s1, s2)
    kio_ref[...] = kir
    kib_ref[...] = kir.astype(BF16)


def _kvpost_call(att, tables, tab_blocks, g, b, layer, tm):
    n = att.shape[0]
    tab = pl.BlockSpec((tm, LANES), lambda i: (i % tab_blocks, 0))
    vec = pl.BlockSpec((1, 1, LANES), lambda i: (layer, 0, 0))
    row = lambda w: pl.BlockSpec((tm, w), lambda i: (i, 0))
    return pl.pallas_call(
        _kvpost_body,
        grid=(n // tm,),
        in_specs=[pl.BlockSpec((tm, KV_DIM), lambda i: (i, ATT_K_BLK)),
                  pl.BlockSpec((tm, KV_DIM), lambda i: (i, ATT_V_BLK)),
                  pl.BlockSpec((tm, IDX_DIM), lambda i: (i, ATT_KI_BLK)),
                  tab, tab, tab, vec, vec],
        out_specs=[row(KV_DIM), row(IDX_DIM), row(KV_DIM), row(KV_DIM), row(IDX_DIM)],
        out_shape=[jax.ShapeDtypeStruct((n, KV_DIM), F32), jax.ShapeDtypeStruct((n, IDX_DIM), F32),
                   jax.ShapeDtypeStruct((n, KV_DIM), BF16), jax.ShapeDtypeStruct((n, KV_DIM), BF16),
                   jax.ShapeDtypeStruct((n, IDX_DIM), BF16)],
        compiler_params=_cp(1),
    )(att, att, att, *tables, g, b)


def _pages_body(pt_ref, *refs, n_steps, pps):
    ck_refs, cv_refs, ci_refs = refs[0:pps], refs[pps:2 * pps], refs[2 * pps:3 * pps]
    kn_ref, vn_ref, in_ref, ko_ref, vo_ref, io_ref = refs[3 * pps:]
    p = pl.program_id(1)

    @pl.when(p < n_steps)
    def _():
        for j in range(pps):
            rows = slice(j * PAGE_SIZE, (j + 1) * PAGE_SIZE)
            for h in range(N_KV_HEADS):
                sl = slice(h * HEAD_DIM, (h + 1) * HEAD_DIM)
                ko_ref[0, rows, sl] = ck_refs[j][:, h, :].astype(BF16)
                vo_ref[0, rows, sl] = cv_refs[j][:, h, :].astype(BF16)
            io_ref[0, rows, :] = ci_refs[j][...].astype(BF16)

    @pl.when(p == n_steps)
    def _():
        r = kn_ref.shape[1]
        tail = pps * PAGE_SIZE - r
        ko_ref[0] = jnp.concatenate([kn_ref[0], jnp.zeros((tail, KV_DIM), F32)], axis=0).astype(BF16)
        vo_ref[0] = jnp.concatenate([vn_ref[0], jnp.zeros((tail, KV_DIM), F32)], axis=0).astype(BF16)
        io_ref[0] = jnp.concatenate([in_ref[0], jnp.zeros((tail, IDX_DIM), F32)], axis=0).astype(BF16)


def _pages_call(page_table, cache_k, cache_v, cache_idx_k, k_new, v_new, ki_new, layer):
    nb, n_pages = page_table.shape
    r = k_new.shape[1]
    pps = math.gcd(n_pages, 8)
    n_steps = n_pages // pps
    blk = pps * PAGE_SIZE
    nk = (n_steps + 1) * blk

    def page_map(j, extra):
        def index_map(b, p, pt):
            return (pt[b * n_pages + jnp.minimum(p, n_steps - 1) * pps + j], layer) + (0,) * extra
        return index_map

    kv_spec = lambda j: pl.BlockSpec((None, None, PAGE_SIZE, N_KV_HEADS, HEAD_DIM), page_map(j, 3))
    ix_spec = lambda j: pl.BlockSpec((None, None, PAGE_SIZE, IDX_DIM), page_map(j, 2))
    new_spec = lambda w: pl.BlockSpec((1, r, w), lambda b, p, pt: (b, 0, 0))
    out_spec = lambda w: pl.BlockSpec((1, blk, w), lambda b, p, pt: (b, p, 0))
    grid_spec = pltpu.PrefetchScalarGridSpec(
        num_scalar_prefetch=1,
        grid=(nb, n_steps + 1),
        in_specs=([kv_spec(j) for j in range(pps)] + [kv_spec(j) for j in range(pps)]
                  + [ix_spec(j) for j in range(pps)] + [new_spec(KV_DIM), new_spec(KV_DIM), new_spec(IDX_DIM)]),
        out_specs=[out_spec(KV_DIM), out_spec(KV_DIM), out_spec(IDX_DIM)],
    )
    return pl.pallas_call(
        functools.partial(_pages_body, n_steps=n_steps, pps=pps),
        grid_spec=grid_spec,
        out_shape=[jax.ShapeDtypeStruct((nb, nk, KV_DIM), BF16), jax.ShapeDtypeStruct((nb, nk, KV_DIM), BF16),
                   jax.ShapeDtypeStruct((nb, nk, IDX_DIM), BF16)],
        compiler_params=_cp(2),
    )(page_table.reshape(-1), *([cache_k] * pps), *([cache_v] * pps), *([cache_idx_k] * pps), k_new, v_new, ki_new)


def _sort_key(score):
    bits = lax.bitcast_convert_type(score, I32)
    return jnp.where(bits < 0, bits ^ jnp.int32(0x7FFFFFFF), bits)


def _kth_largest(key, k):
    kf = jnp.float32(k)

    def count_ge(c):
        return jnp.sum((key >= c).astype(F32), axis=1, keepdims=True)

    int_min = jnp.int32(-2 ** 31)
    prefix = jnp.where(count_ge(jnp.int32(0)) >= kf, jnp.int32(0), int_min)

    def body(i, prefix):
        cand = prefix | lax.shift_left(jnp.int32(1), jnp.int32(30) - i)
        return jnp.where(count_ge(cand) >= kf, cand, prefix)

    return lax.fori_loop(0, 31, body, prefix)


def _select_topk(key, kpos, k):
    thr = _kth_largest(key, k)
    gt = key > thr
    eq = key == thr
    need = jnp.float32(k) - jnp.sum(gt.astype(F32), axis=1, keepdims=True)
    nbits = int(math.ceil(math.log2(key.shape[1] + 1)))

    def body(i, bound):
        cand = bound | lax.shift_left(jnp.int32(1), jnp.int32(nbits - 1) - i)
        cnt = jnp.sum((eq & (kpos < cand)).astype(F32), axis=1, keepdims=True)
        return jnp.where(cnt <= need, cand, bound)

    n_eq = jnp.sum(eq.astype(F32), axis=1, keepdims=True)
    tied = jnp.max(n_eq - need) > 0.0
    full = jnp.full_like(thr, 2 ** nbits - 1)
    bound = lax.cond(tied, lambda: lax.fori_loop(0, nbits, body, jnp.zeros_like(thr)), lambda: full)
    return gt | (eq & (kpos < bound))


def _dsa_body(q_ref, qi_ref, dtwi_ref, c_ref, s1_ref, s2_ref, k_ref, v_ref, ki_ref, o_ref, *, q_pos0, topk):
    tq = q_ref.shape[1]
    nk = k_ref.shape[1]
    c, s1, s2 = c_ref[...], s1_ref[...], s2_ref[...]
    ki = ki_ref[0]
    dtwi = dtwi_ref[0]
    score = jnp.zeros((tq, nk), F32)
    for h in range(IDX_HEADS):
        qh = _rope(qi_ref[0, :, h * IDX_DIM:(h + 1) * IDX_DIM], c, s1, s2).astype(BF16)
        d = _dot_nt(qh, ki)
        score = score + jnp.maximum(d, 0.0) * dtwi[:, WI_LANE + h:WI_LANE + h + 1]
    score = score * IDX_SCALE
    kpos = lax.broadcasted_iota(I32, (tq, nk), 1)
    qpos = q_pos0 + pl.program_id(1) * tq + lax.broadcasted_iota(I32, (tq, nk), 0)
    visible = kpos <= qpos
    score = jnp.where(visible, score, -jnp.inf)
    sel = _select_topk(_sort_key(score), kpos, topk) & visible
    bias = jnp.where(sel, 0.0, -jnp.inf)[None]
    for kvh in range(N_KV_HEADS):
        heads = [kvh * Q_PER_KV + g for g in range(Q_PER_KV)]
        qh = jnp.concatenate(
            [_rope(q_ref[0, :, h * HEAD_DIM:(h + 1) * HEAD_DIM], c, s1, s2) for h in heads], axis=0).astype(BF16)
        sl = slice(kvh * HEAD_DIM, (kvh + 1) * HEAD_DIM)
        logits = _dot_nt(qh, k_ref[0, :, sl]) * ATT_SCALE
        logits = logits.reshape(Q_PER_KV, tq, nk) + bias
        m = jnp.max(logits, axis=-1, keepdims=True)
        p = jnp.exp(logits - m)
        denom = jnp.sum(p, axis=-1, keepdims=True)
        o = _dot(p.astype(BF16).reshape(Q_PER_KV * tq, nk), v_ref[0, :, sl])
        o = o.reshape(Q_PER_KV, tq, HEAD_DIM) / denom
        for g, h in enumerate(heads):
            o_ref[0, :, h * HEAD_DIM:(h + 1) * HEAD_DIM] = o[g].astype(o_ref.dtype)


def _dsa_call(att3, tables, kb, vb, kib, tq, q_pos0, topk, out_dtype, blk0=0, n_blk=None, nk=None):
    nb, tq_total, _ = att3.shape
    nk = kb.shape[1] if nk is None else nk
    n_blk = tq_total // tq if n_blk is None else n_blk
    tab = pl.BlockSpec((tq, LANES), lambda b, j: (blk0 + j, 0))
    return pl.pallas_call(
        functools.partial(_dsa_body, q_pos0=q_pos0 + blk0 * tq, topk=topk),
        grid=(nb, n_blk),
        in_specs=[pl.BlockSpec((1, tq, ATT_DIM), lambda b, j: (b, blk0 + j, 0)),
                  pl.BlockSpec((1, tq, IDX_HEADS * IDX_DIM), lambda b, j: (b, blk0 + j, ATT_QI_BLK)),
                  pl.BlockSpec((1, tq, LANES), lambda b, j: (b, blk0 + j, ATT_DTWI_BLK)),
                  tab, tab, tab,
                  pl.BlockSpec((1, nk, KV_DIM), lambda b, j: (b, 0, 0)),
                  pl.BlockSpec((1, nk, KV_DIM), lambda b, j: (b, 0, 0)),
                  pl.BlockSpec((1, nk, IDX_DIM), lambda b, j: (b, 0, 0))],
        out_specs=pl.BlockSpec((1, tq, ATT_DIM), lambda b, j: (b, j, 0)),
        out_shape=jax.ShapeDtypeStruct((nb, n_blk * tq, ATT_DIM), out_dtype),
        compiler_params=_cp(2),
    )(att3, att3, att3, *tables, kb, vb, kib)


def _pad_rows(x, rows):
    r = x.shape[0]
    if r == rows:
        return x
    return jnp.concatenate([x, jnp.zeros((rows - r,) + x.shape[1:], x.dtype)], axis=0)


def _ssd_body(z_ref, xbc_ref, dtwi_ref, cprev_ref, h0_ref, cw_ref, cb_ref, dtb_ref, alog_ref, dskip_ref, ng_ref,
              y_ref, hout_ref, cnew_ref, xp_ref, ht_ref, *, valid_last, n_chunks):
    ci = pl.program_id(1)
    rows = xbc_ref.shape[1]
    e_heads = SSM_HEADS // SSM_GROUPS
    gw = e_heads * SSM_HEAD_DIM
    pre = 8

    @pl.when(ci == 0)
    def _():
        xp_ref[0:pre, :] = _pad_rows(jnp.concatenate(
            [jnp.zeros((pre - (CONV_W - 1), XBC_DIM), F32), cprev_ref[0]], axis=0), pre)
        for g in range(SSM_GROUPS):
            ht_ref[g] = h0_ref[0, g * gw:(g + 1) * gw, :].T

    xp_ref[pre:pre + CHUNK, :] = _pad_rows(xbc_ref[0], CHUNK)
    conv = cb_ref[0]
    for i in range(CONV_W):
        off = pre - (CONV_W - 1) + i
        conv = conv + xp_ref[off:off + CHUNK, :] * cw_ref[0, i:i + 1, :]
    last = valid_last if n_chunks == 1 else CHUNK

    @pl.when(ci == n_chunks - 1)
    def _():
        cnew_ref[0] = xp_ref[pre + valid_last - (CONV_W - 1):pre + valid_last, :]

    if n_chunks > 1:
        tail = xp_ref[pre + last - (CONV_W - 1):pre + last, :]
        xp_ref[pre - (CONV_W - 1):pre, :] = tail

    u = _silu(conv)
    xs = u[:, :SSM_DIM]
    bm = u[:, SSM_DIM:SSM_DIM + SSM_GROUPS * SSM_STATE]
    cm = u[:, SSM_DIM + SSM_GROUPS * SSM_STATE:]

    dt_raw = _pad_rows(dtwi_ref[0], CHUNK)
    lane = lax.broadcasted_iota(I32, (CHUNK, LANES), 1)
    rowi = lax.broadcasted_iota(I32, (CHUNK, LANES), 0)
    xdt_in = dt_raw + dtb_ref[0]
    dt = jnp.maximum(xdt_in, 0.0) + jnp.log1p(jnp.exp(-jnp.abs(xdt_in)))
    row_ok = rowi < jnp.where(ci == n_chunks - 1, valid_last, CHUNK)
    dt = jnp.where((lane < SSM_HEADS) & row_ok, dt, 0.0)
    a = -jnp.exp(alog_ref[0])
    da = dt * a
    tri = (lax.broadcasted_iota(I32, (CHUNK, CHUNK), 1) <= lax.broadcasted_iota(I32, (CHUNK, CHUNK), 0))
    cum = _dot3_left(tri.astype(BF16), da)
    cum_t = cum.T
    hh = lax.broadcasted_iota(I32, (LANES, SSM_DIM), 0)
    cc = lax.broadcasted_iota(I32, (LANES, SSM_DIM), 1)
    expand = (cc // SSM_HEAD_DIM == hh).astype(BF16)
    dt_x = _dot3_right(dt, expand)
    cum_x = _dot3_right(cum, expand)
    cum_last = cum[CHUNK - 1:CHUNK, :]
    cum_last_x = cum_x[CHUNK - 1:CHUNK, :]
    xdt = xs * dt_x
    grow = jnp.exp(cum_x)
    xdt_end = (xdt * jnp.exp(cum_last_x - cum_x)).astype(BF16)
    xdt_b = xdt.astype(BF16)
    lane_lo = lax.broadcasted_iota(I32, (CHUNK, LANES), 1) < SSM_HEAD_DIM

    for g in range(SSM_GROUPS):
        st = slice(g * SSM_STATE, (g + 1) * SSM_STATE)
        ch = slice(g * gw, (g + 1) * gw)
        bm_g = bm[:, st].astype(BF16)
        cm_g = cm[:, st].astype(BF16)
        cb = _dot_nt(cm_g, bm_g)
        h_prev = ht_ref[g]
        y_off = _dot(cm_g, h_prev.astype(BF16)) * grow[:, ch]
        parts = []
        for pr in range(e_heads // 2):
            acc = None
            pair = slice(g * gw + pr * LANES, g * gw + (pr + 1) * LANES)
            xp_pair = xdt_b[:, pair]
            for half in range(2):
                h = g * e_heads + pr * 2 + half
                seg = cum[:, h:h + 1] - cum_t[h:h + 1, :]
                dec = jnp.exp(jnp.where(tri, seg, -jnp.inf))
                m = (cb * dec).astype(BF16)
                keep = lane_lo if half == 0 else jnp.logical_not(lane_lo)
                part = _dot(m, jnp.where(keep, xp_pair, jnp.zeros_like(xp_pair)))
                acc = part if acc is None else acc + part
            parts.append(acc)
        y_g = jnp.concatenate(parts, axis=1) + y_off + dskip_ref[0, :, ch] * xs[:, ch]
        y_g = y_g * _silu(z_pad_slice(z_ref, ch))
        ms = jnp.mean(jnp.square(y_g), axis=-1, keepdims=True)
        y_g = y_g * lax.rsqrt(ms + RMS_EPS) * ng_ref[0, :, ch]
        y_ref[0, :, ch] = y_g[:rows].astype(y_ref.dtype)
        states_t = _dot(bm_g.astype(F32).T.astype(BF16), xdt_end[:, ch])
        ht_ref[g] = h_prev * jnp.exp(cum_last_x[:, ch]) + states_t

    @pl.when(ci == n_chunks - 1)
    def _():
        for g in range(SSM_GROUPS):
            hout_ref[0, g * gw:(g + 1) * gw, :] = ht_ref[g].T


def z_pad_slice(z_ref, ch):
    return _pad_rows(z_ref[0, :, ch], CHUNK)


def _ssd_call(z3, xbc3, att3, conv_prev, h0, layer, params, valid_last, out_dtype, h0_layered):
    nb, t, _ = z3.shape
    rows = min(t, CHUNK)
    n_chunks = t // rows
    conv_w, conv_b, dt_bias, a_log, d_skip, norm_g = params
    hp = SSM_HEADS * SSM_HEAD_DIM
    if h0_layered:
        cprev_spec = pl.BlockSpec((1, None, CONV_W - 1, XBC_DIM), lambda b, c: (b, layer, 0, 0))
        h0_spec = pl.BlockSpec((1, None, hp, SSM_STATE), lambda b, c: (b, layer, 0, 0))
    else:
        cprev_spec = pl.BlockSpec((1, CONV_W - 1, XBC_DIM), lambda b, c: (b, 0, 0))
        h0_spec = pl.BlockSpec((1, hp, SSM_STATE), lambda b, c: (b, 0, 0))
    lvec = lambda w: pl.BlockSpec((1, 1, w), lambda b, c: (layer, 0, 0))
    return pl.pallas_call(
        functools.partial(_ssd_body, valid_last=valid_last, n_chunks=n_chunks),
        grid=(nb, n_chunks),
        in_specs=[pl.BlockSpec((1, rows, SSM_DIM), lambda b, c: (b, c, 0)),
                  pl.BlockSpec((1, rows, XBC_DIM), lambda b, c: (b, c, 0)),
                  pl.BlockSpec((1, rows, LANES), lambda b, c: (b, c, ATT_DTWI_BLK)),
                  cprev_spec, h0_spec,
                  pl.BlockSpec((1, 8, XBC_DIM), lambda b, c: (layer, 0, 0)),
                  lvec(XBC_DIM), lvec(LANES), lvec(LANES), lvec(SSM_DIM), lvec(SSM_DIM)],
        out_specs=[pl.BlockSpec((1, rows, SSM_DIM), lambda b, c: (b, c, 0)),
                   pl.BlockSpec((1, hp, SSM_STATE), lambda b, c: (b, 0, 0)),
                   pl.BlockSpec((1, CONV_W - 1, XBC_DIM), lambda b, c: (b, 0, 0))],
        out_shape=[jax.ShapeDtypeStruct((nb, t, SSM_DIM), out_dtype),
                   jax.ShapeDtypeStruct((nb, hp, SSM_STATE), F32),
                   jax.ShapeDtypeStruct((nb, CONV_W - 1, XBC_DIM), F32)],
        scratch_shapes=[pltpu.VMEM((8 + CHUNK, XBC_DIM), F32),
                        pltpu.VMEM((SSM_GROUPS, SSM_STATE, hp // SSM_GROUPS), F32)],
        compiler_params=_cp(2),
    )(z3, xbc3, att3, conv_prev, h0, conv_w, conv_b, dt_bias, a_log, d_skip, norm_g)


def _gelu_tanh(x):
    return 0.5 * x * (1.0 + jnp.tanh(math.sqrt(2.0 / math.pi) * (x + 0.044715 * (x * x * x))))


def _gmlp_body(gm_ref, lg_ref, lb_ref, ws_ref, bs_ref, o_ref, v_ref):
    rows = gm_ref.shape[1]
    g = _gelu_tanh(gm_ref[0])
    u = g[:, :GM_DIM]
    v = g[:, GM_DIM:]
    mu = jnp.mean(v, axis=-1, keepdims=True)
    var = jnp.mean(jnp.square(v - mu), axis=-1, keepdims=True)
    v = (v - mu) * lax.rsqrt(var + LN_EPS) * lg_ref[0] + lb_ref[0]
    v_ref[0] = v
    vb = _pad_rows(v, CHUNK).astype(BF16)
    causal = (lax.broadcasted_iota(I32, (CHUNK, CHUNK), 1) <= lax.broadcasted_iota(I32, (CHUNK, CHUNK), 0))
    gd = GM_DIM // GM_GROUPS
    bs = bs_ref[0]
    for grp in range(GM_GROUPS):
        sl = slice(grp * gd, (grp + 1) * gd)
        wm = jnp.where(causal, ws_ref[0, grp], 0.0).astype(BF16)
        mix = _dot(wm, vb[:, sl]) + bs[:, grp:grp + 1]
        o_ref[0, :, sl] = (u[:, sl] * mix[:rows]).astype(o_ref.dtype)


def _gmlp_call(gm3, layer, params, out_dtype):
    nb, t, _ = gm3.shape
    rows = min(t, CHUNK)
    ln_g, ln_b, ws, bs_t = params
    lvec = lambda w: pl.BlockSpec((1, 1, w), lambda b, c: (layer, 0, 0))
    return pl.pallas_call(
        _gmlp_body,
        grid=(nb, t // rows),
        in_specs=[pl.BlockSpec((1, rows, 2 * GM_DIM), lambda b, c: (b, c, 0)),
                  lvec(GM_DIM), lvec(GM_DIM),
                  pl.BlockSpec((1, GM_GROUPS, CHUNK, CHUNK), lambda b, c: (layer, 0, 0, 0)),
                  pl.BlockSpec((1, CHUNK, LANES), lambda b, c: (layer, 0, 0))],
        out_specs=[pl.BlockSpec((1, rows, GM_DIM), lambda b, c: (b, c, 0)),
                   pl.BlockSpec((1, rows, GM_DIM), lambda b, c: (b, c, 0))],
        out_shape=[jax.ShapeDtypeStruct((nb, t, GM_DIM), out_dtype),
                   jax.ShapeDtypeStruct((nb, t, GM_DIM), F32)],
        compiler_params=_cp(2),
    )(gm3, ln_g, ln_b, ws, bs_t)


def _merge_body(h_ref, oa_ref, ob_ref, oc_ref, wga_ref, wgb_ref, wgc_ref, wba_ref, wbb_ref, wbc_ref, m_ref):
    h = h_ref[...]
    acc = None
    for o_ref, wg_ref, wb_ref in ((oa_ref, wga_ref, wba_ref), (ob_ref, wgb_ref, wbb_ref), (oc_ref, wgc_ref, wbc_ref)):
        term = jax.nn.sigmoid(_dot(h, wg_ref[0])) * _dot(o_ref[...], wb_ref[0, 0])
        acc = term if acc is None else acc + term
    m_ref[...] = acc.astype(m_ref.dtype)


def _merge_call(h, oa, ob, oc, w_gate, w_branch, layer, tm, tn):
    n, d = h.shape
    nj = d // tn
    row = lambda w: pl.BlockSpec((tm, w), lambda i, j: (i, 0))
    wg = lambda br: pl.BlockSpec((1, d, tn), lambda i, j: (layer, 0, br * nj + j))
    wb = lambda br: pl.BlockSpec((1, 1, BRANCH_DIM, tn), lambda i, j: (layer, br, 0, j))
    return pl.pallas_call(
        _merge_body,
        grid=(n // tm, nj),
        in_specs=[row(d), row(BRANCH_DIM), row(BRANCH_DIM), row(BRANCH_DIM),
                  wg(0), wg(1), wg(2), wb(0), wb(1), wb(2)],
        out_specs=pl.BlockSpec((tm, tn), lambda i, j: (i, j)),
        out_shape=jax.ShapeDtypeStruct((n, d), BF16),
        compiler_params=_cp(2),
    )(h, oa, ob, oc, w_gate, w_gate, w_gate, w_branch, w_branch, w_branch)


def _ln_finish(acc_ref, nj, tn, lg_ref, lb_ref, sc_ref, sh_ref, xo_ref, ho_ref, po_ref, router=None):
    tm = acc_ref.shape[1]
    d = nj * tn
    s = jnp.zeros((tm, 1), F32)
    for jj in range(nj):
        s = s + jnp.sum(acc_ref[jj], axis=1, keepdims=True)
    mu = s / d
    v = jnp.zeros((tm, 1), F32)
    for jj in range(nj):
        dlt = acc_ref[jj] - mu
        v = v + jnp.sum(dlt * dlt, axis=1, keepdims=True)
    rstd = lax.rsqrt(v / d + LN_EPS)

    def finish(jj):
        sl = slice(jj * tn, (jj + 1) * tn)
        y = (acc_ref[jj] - mu) * rstd * lg_ref[0, :, sl] + lb_ref[0, :, sl]
        xo_ref[:, sl] = y
        hmod = y * (1.0 + sc_ref[0, :, sl]) + sh_ref[0, :, sl]
        hb = hmod.astype(BF16)
        if ho_ref is not None:
            ho_ref[:, sl] = hb
        return hb

    logits = None
    for jj in range(nj // 2):
        lo = finish(jj)
        hi = finish(jj + nj // 2)
        if po_ref is not None:
            _pack_store(po_ref, lo, hi, jj * (tn // LANES), tm, d // 2 // LANES)
        if router is not None:
            wr_ref = router[0]
            part = (_dot(lo, wr_ref[0, jj * tn:(jj + 1) * tn, :])
                    + _dot(hi, wr_ref[0, (jj + nj // 2) * tn:(jj + nj // 2 + 1) * tn, :]))
            logits = part if logits is None else logits + part
    if router is not None:
        _, br_ref, ti_ref, tw_ref = router
        ti_ref[...], tw_ref[...] = _route(logits + br_ref[0])


def _pack_store(po_ref, lo, hi, group0, tm, groups):
    lo_bits = lax.shift_right_logical(lax.bitcast_convert_type(lo.astype(F32), I32), jnp.int32(16))
    hi_bits = lax.bitcast_convert_type(hi.astype(F32), I32) & jnp.int32(-65536)
    word = lo_bits | hi_bits
    for g in range(word.shape[1] // LANES):
        po_ref[pl.ds(group0 + g, tm, stride=groups), :] = word[:, g * LANES:(g + 1) * LANES]


def _projln_body(a_ref, w_ref, x_ref, g_ref, lg_ref, lb_ref, sc_ref, sh_ref, wr_ref, br_ref,
                 xo_ref, po_ref, ti_ref, tw_ref, acc_ref, *, nj, tn, alpha):
    j = pl.program_id(1)
    acc_ref[j] = _dot(a_ref[...], w_ref[0])

    @pl.when(j == nj - 1)
    def _():
        for jj in range(nj):
            sl = slice(jj * tn, (jj + 1) * tn)
            acc_ref[jj] = alpha * x_ref[:, sl] + g_ref[0, :, sl] * acc_ref[jj]
        _ln_finish(acc_ref, nj, tn, lg_ref, lb_ref, sc_ref, sh_ref, xo_ref, None, po_ref,
                   router=(wr_ref, br_ref, ti_ref, tw_ref))


def _projln_call(grp, a, w_out, x, gate, ln_g, ln_b, sc, sh, w_router, b_router, layer, alpha, tn):
    n, d = x.shape
    tm = grp.tm
    nj = d // tn
    row = lambda w: pl.BlockSpec((tm, w), lambda i, j: (i, 0))
    lvec = pl.BlockSpec((1, 1, d), lambda i, j: (layer, 0, 0))
    return pl.pallas_call(
        functools.partial(_projln_body, nj=nj, tn=tn, alpha=alpha),
        grid=(n // tm, nj),
        in_specs=[row(d), pl.BlockSpec((1, d, tn), lambda i, j: (layer, 0, j)), row(d),
                  grp.mod_spec(), lvec, lvec, grp.mod_spec(), grp.mod_spec(),
                  pl.BlockSpec((1, d, LANES), lambda i, j: (layer, 0, 0)),
                  pl.BlockSpec((1, 1, LANES), lambda i, j: (layer, 0, 0))],
        out_specs=[row(d), pl.BlockSpec((tm * (d // 2 // LANES), LANES), lambda i, j: (i, 0)),
                   row(LANES), row(LANES)],
        out_shape=[jax.ShapeDtypeStruct((n, d), F32), jax.ShapeDtypeStruct((n * (d // 2 // LANES), LANES), I32),
                   jax.ShapeDtypeStruct((n, LANES), I32), jax.ShapeDtypeStruct((n, LANES), F32)],
        scratch_shapes=[pltpu.VMEM((nj, tm, tn), F32)],
        compiler_params=_cp(2),
    )(a, w_out, x, gate, ln_g, ln_b, sc, sh, w_router, b_router)


def _combineln_body(y_ref, tw_ref, x_ref, g_ref, lg_ref, lb_ref, sc_ref, sh_ref, xo_ref, ho_ref, acc_ref,
                    *, nj, tn, alpha):
    tw = tw_ref[...]
    for jj in range(nj):
        sl = slice(jj * tn, (jj + 1) * tn)
        moe = y_ref[0, :, sl] * tw[:, 0:1]
        for k in range(1, TOP_K):
            moe = moe + y_ref[k, :, sl] * tw[:, k:k + 1]
        acc_ref[jj] = alpha * x_ref[:, sl] + g_ref[0, :, sl] * moe
    _ln_finish(acc_ref, nj, tn, lg_ref, lb_ref, sc_ref, sh_ref, xo_ref, ho_ref, None)


def _combineln_call(grp, y4, row0, tw, x, gate, ln_g, ln_b, sc, sh, layer, alpha, tm):
    n, d = x.shape
    tn = 512
    nj = d // tn
    rb = row0 // tm
    g2 = _Group(grp.n, tm, grp.rows_per_seq, grp.per_row)
    row = lambda w: pl.BlockSpec((tm, w), lambda i: (i, 0))
    lvec = pl.BlockSpec((1, 1, d), lambda i: (layer, 0, 0))
    return pl.pallas_call(
        functools.partial(_combineln_body, nj=nj, tn=tn, alpha=alpha),
        grid=(n // tm,),
        in_specs=[pl.BlockSpec((TOP_K, tm, d), lambda i: (0, rb + i, 0)),
                  pl.BlockSpec((tm, LANES), lambda i: (rb + i, 0)), row(d),
                  g2.mod_spec(), lvec, lvec, g2.mod_spec(), g2.mod_spec()],
        out_specs=[row(d), row(d)],
        out_shape=[jax.ShapeDtypeStruct((n, d), F32), jax.ShapeDtypeStruct((n, d), BF16)],
        scratch_shapes=[pltpu.VMEM((nj, tm, tn), F32)],
        compiler_params=_cp(1),
    )(y4, tw, x, gate, ln_g, ln_b, sc, sh)


def _route(lg):
    tm = lg.shape[0]
    lane = lax.broadcasted_iota(I32, (tm, LANES), 1)
    lg = jnp.where(lane < N_EXPERTS, lg, -jnp.inf)
    vals, idxs = [], []
    for _ in range(TOP_K):
        m = jnp.max(lg, axis=1, keepdims=True)
        idx = jnp.min(jnp.where(lg == m, lane, LANES), axis=1, keepdims=True)
        vals.append(m)
        idxs.append(idx)
        lg = jnp.where(lane == idx, -jnp.inf, lg)
    es = [jnp.exp(v - vals[0]) for v in vals]
    tot = es[0] + es[1] + es[2] + es[3]
    ti = jnp.zeros((tm, LANES), I32)
    tw = jnp.zeros((tm, LANES), F32)
    for k in range(TOP_K):
        ti = jnp.where(lane == k, idxs[k], ti)
        tw = jnp.where(lane == k, es[k] / tot, tw)
    return ti, tw


def _rank_body(ti_ref, rank_ref, cnt_ref, carry_ref):
    i = pl.program_id(0)

    @pl.when(i == 0)
    def _():
        carry_ref[...] = jnp.zeros_like(carry_ref)

    ti = ti_ref[...]
    tm = ti.shape[0]
    lane = lax.broadcasted_iota(I32, (tm, LANES), 1)
    onehot = jnp.zeros((tm, LANES), F32)
    for k in range(TOP_K):
        onehot = onehot + (lane == ti[:, k:k + 1]).astype(F32)
    strict = (lax.broadcasted_iota(I32, (tm, tm), 1) < lax.broadcasted_iota(I32, (tm, tm), 0)).astype(BF16)
    before = _dot(strict, onehot.astype(BF16)) + carry_ref[0:1, :]
    rank = jnp.zeros((tm, LANES), F32)
    for k in range(TOP_K):
        rk = jnp.sum(jnp.where(lane == ti[:, k:k + 1], before, 0.0), axis=1, keepdims=True)
        rank = jnp.where(lane == k, rk, rank)
    rank_ref[...] = rank.astype(I32)
    total = carry_ref[0:1, :] + jnp.sum(onehot, axis=0, keepdims=True)
    carry_ref[...] = jnp.broadcast_to(total, carry_ref.shape)
    cnt_ref[...] = jnp.broadcast_to(total, cnt_ref.shape).astype(I32)


def _rank_call(ti, tm):
    n = ti.shape[0]
    return pl.pallas_call(
        _rank_body,
        grid=(n // tm,),
        in_specs=[pl.BlockSpec((tm, LANES), lambda i: (i, 0))],
        out_specs=[pl.BlockSpec((tm, LANES), lambda i: (i, 0)), pl.BlockSpec((8, LANES), lambda i: (0, 0))],
        out_shape=[jax.ShapeDtypeStruct((n, LANES), I32), jax.ShapeDtypeStruct((8, LANES), I32)],
        scratch_shapes=[pltpu.VMEM((8, LANES), F32)],
        compiler_params=_cp(1),
    )(ti)


def _dispatch_body(dst_idx_ref, hp_ref, xs_init_ref, xs_ref, sem, *, tokens, rows):
    del xs_init_ref

    def slot_copy(i, d):
        t = lax.shift_right_logical(i, 2)
        return pltpu.make_async_copy(hp_ref.at[pl.ds(pl.multiple_of(t * rows, rows), rows)],
                                     xs_ref.at[pl.ds(pl.multiple_of(d * rows, rows), rows)], sem)

    def issue(i, carry):
        slot_copy(i, dst_idx_ref[i]).start()
        return carry

    lax.fori_loop(0, tokens * TOP_K, issue, 0, unroll=8)

    def wait(i, carry):
        slot_copy(i, 0).wait()
        return carry

    lax.fori_loop(0, tokens * TOP_K, wait, 0, unroll=8)


def _dispatch_call(dst_idx, hp, cap, rows, tokens):
    assert TOP_K == 4
    n = hp.shape[0] // rows
    shape = (cap * rows, LANES)
    any_spec = pl.BlockSpec(memory_space=pl.ANY)
    return pl.pallas_call(
        functools.partial(_dispatch_body, tokens=tokens, rows=rows),
        grid=(n // tokens,),
        in_specs=[pl.BlockSpec((tokens * TOP_K,), lambda i: (i,), memory_space=pltpu.SMEM),
                  pl.BlockSpec((tokens * rows, LANES), lambda i: (i, 0)), any_spec],
        out_specs=any_spec,
        out_shape=jax.ShapeDtypeStruct(shape, hp.dtype),
        scratch_shapes=[pltpu.SemaphoreType.DMA(())],
        input_output_aliases={2: 0},
        compiler_params=pltpu.CompilerParams(dimension_semantics=("arbitrary",), has_side_effects=True,
                                             disable_bounds_checks=True),
    )(dst_idx, hp, jnp.zeros(shape, hp.dtype))


def _collect_body(src_idx_ref, ys_ref, y4_ref, sem, *, tokens):
    def slot_copy(i, r):
        t = lax.shift_right_logical(i, 2)
        k = i & 3
        return pltpu.make_async_copy(ys_ref.at[pl.ds(r, 1)], y4_ref.at[k, pl.ds(t, 1)], sem)

    def issue(i, carry):
        slot_copy(i, src_idx_ref[i]).start()
        return carry

    lax.fori_loop(0, tokens * TOP_K, issue, 0, unroll=8)

    def wait(i, carry):
        slot_copy(i, 0).wait()
        return carry

    lax.fori_loop(0, tokens * TOP_K, wait, 0, unroll=8)


def _collect_call(src_idx, ys, n, tokens):
    assert TOP_K == 4
    d = ys.shape[1]
    return pl.pallas_call(
        functools.partial(_collect_body, tokens=tokens),
        grid=(n // tokens,),
        in_specs=[pl.BlockSpec((tokens * TOP_K,), lambda i: (i,), memory_space=pltpu.SMEM),
                  pl.BlockSpec(memory_space=pl.ANY)],
        out_specs=pl.BlockSpec((TOP_K, tokens, d), lambda i: (0, i, 0)),
        out_shape=jax.ShapeDtypeStruct((TOP_K, n, d), ys.dtype),
        scratch_shapes=[pltpu.SemaphoreType.DMA(())],
        compiler_params=pltpu.CompilerParams(dimension_semantics=("arbitrary",), has_side_effects=True,
                                             disable_bounds_checks=True),
    )(src_idx, ys)


def _unpack(xp):
    lo = lax.bitcast_convert_type(lax.shift_left(xp, jnp.int32(16)), F32).astype(BF16)
    hi = lax.bitcast_convert_type(xp & jnp.int32(-65536), F32).astype(BF16)
    return lo, hi


def _expert1_body(blk_ref, e_ref, f_ref, wf_ref, first_ref, n_ref, xs_ref, wg_ref, wl_ref, bg_ref, bl_ref, act_ref,
                  wgb_ref, wlb_ref):
    s = pl.program_id(0)

    @pl.when(s < n_ref[0])
    def _():
        @pl.when(first_ref[s] == 1)
        def _():
            wgb_ref[...] = wg_ref[0, 0].astype(BF16)
            wlb_ref[...] = wl_ref[0, 0].astype(BF16)

        groups = xs_ref.shape[0] // MOE_TM
        words = jnp.concatenate([xs_ref[pl.ds(g, MOE_TM, stride=groups), :] for g in range(groups)], axis=1)
        lo, hi = _unpack(words)
        half = lo.shape[1]
        glu = _dot(lo, wgb_ref[0:half, :]) + _dot(hi, wgb_ref[half:, :]) + bg_ref[0, 0]
        lin = _dot(lo, wlb_ref[0:half, :]) + _dot(hi, wlb_ref[half:, :]) + bl_ref[0, 0]
        glu = jnp.minimum(glu, SWIGLU_LIMIT)
        lin = jnp.clip(lin, -SWIGLU_LIMIT, SWIGLU_LIMIT)
        act_ref[...] = (glu * jax.nn.sigmoid(SWIGLU_ALPHA * glu) * (lin + 1.0)).astype(BF16)

    @pl.when(s >= n_ref[0])
    def _():
        act_ref[...] = jnp.zeros_like(act_ref)


def _expert1_call(plan, xs, w1, b1, layer):
    d = w1.shape[2]
    groups = d // 2 // LANES
    cap = xs.shape[0] // groups
    n_steps_max = plan["step_blk"].shape[0]
    nf = EXPERT_DIM // MOE_TF
    grid_spec = pltpu.PrefetchScalarGridSpec(
        num_scalar_prefetch=6,
        grid=(n_steps_max,),
        in_specs=[pl.BlockSpec((MOE_TM * groups, LANES), lambda s, blk, e, f, wf, fi, n: (blk[s], 0)),
                  pl.BlockSpec((1, 1, d, MOE_TF), lambda s, blk, e, f, wf, fi, n: (layer, e[s], 0, wf[s])),
                  pl.BlockSpec((1, 1, d, MOE_TF), lambda s, blk, e, f, wf, fi, n: (layer, e[s], 0, nf + wf[s])),
                  pl.BlockSpec((1, 1, 1, MOE_TF), lambda s, blk, e, f, wf, fi, n: (layer, e[s], 0, wf[s])),
                  pl.BlockSpec((1, 1, 1, MOE_TF), lambda s, blk, e, f, wf, fi, n: (layer, e[s], 0, nf + wf[s]))],
        out_specs=pl.BlockSpec((MOE_TM, MOE_TF), lambda s, blk, e, f, wf, fi, n: (blk[s], f[s])),
        scratch_shapes=[pltpu.VMEM((d, MOE_TF), BF16), pltpu.VMEM((d, MOE_TF), BF16)],
    )
    return pl.pallas_call(
        _expert1_body,
        grid_spec=grid_spec,
        out_shape=jax.ShapeDtypeStruct((cap, EXPERT_DIM), BF16),
        compiler_params=_cp(1),
    )(plan["step_blk"], plan["step_e"], plan["step_f"], plan["step_wf"], plan["step_first"], plan["n_steps"],
      xs, w1, w1, b1, b1)


def _expert2_body(e_ref, first_ref, n_ref, act_ref, w_ref, b_ref, y_ref, wb_ref):
    i = pl.program_id(0)

    @pl.when(i < n_ref[0])
    def _():
        @pl.when(first_ref[i] == 1)
        def _():
            wb_ref[...] = w_ref[0, 0].astype(BF16)

        y_ref[...] = _dot(act_ref[...], wb_ref[...]) + b_ref[0, 0]

    @pl.when(i >= n_ref[0])
    def _():
        y_ref[...] = jnp.zeros_like(y_ref)


def _expert2_call(plan, act, w2, b2, layer):
    cap, f = act.shape
    d = w2.shape[-1]
    nblk = cap // MOE_TM
    grid_spec = pltpu.PrefetchScalarGridSpec(
        num_scalar_prefetch=3,
        grid=(nblk,),
        in_specs=[pl.BlockSpec((MOE_TM, f), lambda i, e, fi, n: (i, 0)),
                  pl.BlockSpec((1, 1, f, d), lambda i, e, fi, n: (layer, e[i], 0, 0)),
                  pl.BlockSpec((1, 1, 1, d), lambda i, e, fi, n: (layer, e[i], 0, 0))],
        out_specs=pl.BlockSpec((MOE_TM, d), lambda i, e, fi, n: (i, 0)),
        scratch_shapes=[pltpu.VMEM((f, d), BF16)],
    )
    return pl.pallas_call(
        _expert2_body,
        grid_spec=grid_spec,
        out_shape=jax.ShapeDtypeStruct((cap, d), F32),
        compiler_params=_cp(1),
    )(plan["blk_e"], plan["blk_first"], plan["n_blk"], act, w2, b2)


def _moe_plan(counts, top_i, rank, nblk_max):
    nb = (counts + MOE_TM - 1) // MOE_TM
    pad_end = jnp.cumsum(nb * MOE_TM)
    offsets = pad_end - nb * MOE_TM
    cumblk = jnp.cumsum(nb)
    blk_start = cumblk - nb
    n_blk = cumblk[-1]
    dest = jnp.take(offsets, top_i) + rank
    last = N_EXPERTS - 1
    bi = jnp.minimum(jnp.arange(nblk_max, dtype=I32), n_blk - 1)
    count_le = lambda ends, v: jnp.sum((ends[None, :] <= v[:, None]).astype(I32), axis=1)
    blk_e = jnp.minimum(count_le(cumblk, bi), last).astype(I32)
    blk_first = (bi == jnp.take(blk_start, blk_e)).astype(I32)
    nf = EXPERT_DIM // MOE_TF
    step_end = jnp.cumsum(nf * nb)
    n_steps = step_end[-1]
    s_all = jnp.arange(nf * nblk_max, dtype=I32)
    si = jnp.minimum(s_all, n_steps - 1)
    se = jnp.minimum(count_le(step_end, si), last).astype(I32)
    nbe = jnp.maximum(jnp.take(nb, se), 1)
    local = si - (jnp.take(step_end, se) - nf * jnp.take(nb, se))
    spare = s_all - n_steps
    swf = (local // nbe).astype(I32)
    sf = jnp.where(spare >= 0, spare % nf, swf).astype(I32)
    sblk = jnp.where(spare >= 0, n_blk + spare // nf, jnp.take(blk_start, se) + local % nbe).astype(I32)
    sfirst = (local % nbe == 0).astype(I32)
    return dict(dest=dest.astype(I32), blk_e=blk_e, blk_first=blk_first, n_blk=n_blk.reshape(1).astype(I32),
                step_blk=sblk, step_e=se, step_f=sf, step_wf=swf, step_first=sfirst,
                n_steps=n_steps.reshape(1).astype(I32))


def _moe(hp_all, ti_all, w1, b1, w2, b2, layer):
    n = ti_all.shape[0]
    d = w1.shape[2]
    rank_all, counts8 = _rank_call(ti_all, 64)
    counts = counts8[0, :N_EXPERTS]
    n_assign = n * TOP_K
    nblk_max = -(-(n_assign + N_EXPERTS * (MOE_TM - 1)) // MOE_TM)
    cap = nblk_max * MOE_TM
    plan = _moe_plan(counts, ti_all[:, :TOP_K], rank_all[:, :TOP_K], nblk_max)
    dest = plan["dest"].reshape(-1)
    tokens = 64
    xs = _dispatch_call(dest, hp_all, cap, d // 2 // LANES, tokens)
    act = _expert1_call(plan, xs, w1, b1, layer)
    ys = _expert2_call(plan, act, w2, b2, layer)
    return _collect_call(dest, ys, n, tokens)


def _mixers(grp, h, att, z, xbc, gm, nb, t, layer, prm, tables, tab_blocks, past, ssm_state, out_dtype):
    k_f, ki_f, kb, vb, kib = _kvpost_call(att, tables, tab_blocks, prm["idx_kn_g"], prm["idx_kn_b"], layer,
                                          min(grp.n, 512))
    att3 = att.reshape(nb, t, ATT_COLS)
    if past is None:
        nk = t
        q_pos0 = 0
        kb3, vb3, kib3 = (a.reshape(nb, t, -1) for a in (kb, vb, kib))
        tq = CHUNK
    else:
        page_table, cache_k, cache_v, cache_idx_k = past
        v_f = att[:, ATT_V_BLK * KV_DIM:(ATT_V_BLK + 1) * KV_DIM]
        kb3, vb3, kib3 = _pages_call(page_table, cache_k, cache_v, cache_idx_k, k_f.reshape(nb, t, KV_DIM),
                                     v_f.reshape(nb, t, KV_DIM), ki_f.reshape(nb, t, IDX_DIM), layer)
        q_pos0 = page_table.shape[1] * PAGE_SIZE
        nk = q_pos0 + prm["t_valid"]
        tq = t
    topk = min(TOPK_MAX, nk // 4)
    if past is None:
        n_qblk = t // tq
        per_call = 2 if n_qblk % 2 == 0 else 1
        o_a = jnp.concatenate(
            [_dsa_call(att3, tables, kb3, vb3, kib3, tq, q_pos0, topk, out_dtype, blk0=b0, n_blk=per_call,
                       nk=(b0 + per_call) * tq) for b0 in range(0, n_qblk, per_call)], axis=1)
    else:
        o_a = _dsa_call(att3, tables, kb3, vb3, kib3, tq, q_pos0, topk, out_dtype)
    ssm_params = (prm["conv_w"], prm["conv_b"], prm["dt_bias"], prm["a_log"], prm["d_skip"], prm["ssm_norm_g"])
    conv_prev, h0, layered = ssm_state
    valid_last = prm["t_valid"] if t < CHUNK else CHUNK
    o_b, ssm_new, conv_new = _ssd_call(z.reshape(nb, t, SSM_DIM), xbc.reshape(nb, t, XBC_DIM), att3, conv_prev, h0,
                                       layer, ssm_params, valid_last, out_dtype, layered)
    gm_params = (prm["gm_ln_g"], prm["gm_ln_b"], prm["gm_ws"], prm["gm_bs_t"])
    o_c, gm_v = _gmlp_call(gm.reshape(nb, t, 2 * GM_DIM), layer, gm_params, out_dtype)
    flat = lambda a: a.reshape(nb * t, -1).astype(BF16)
    return (flat(o_a), flat(o_b), flat(o_c)), (k_f, ki_f, ssm_new, conv_new, gm_v)


def kernel(x_prompt, x_sample, cache_k, cache_v, cache_idx_k, state_ssm, state_conv, page_table, c_prompt, c_sample, ln1_g, ln1_b, ln2_g, ln2_b, w_ada, b_ada, w_in, idx_kn_g, idx_kn_b, conv_w, conv_b, dt_bias, a_log, d_skip, ssm_norm_g, gm_ln_g, gm_ln_b, gm_ws, gm_bs, w_gate, w_branch, w_out, w_router, b_router, w1, b1, w2, b2):
    bp, sp, d = x_prompt.shape
    db, ds, _ = x_sample.shape
    depth = w_in.shape[0]
    alpha = (2 * depth) ** 0.25
    past_len = page_table.shape[1] * PAGE_SIZE
    n_p, n_s = bp * sp, db * SAMPLE_ROWS
    grp_p = _Group(n_p, 256, sp, False)
    grp_s = _Group(n_s, n_s, SAMPLE_ROWS, True)

    o = np.cumsum([0, ATT_DIM, KV_DIM, KV_DIM, IDX_HEADS * IDX_DIM, IDX_DIM, IDX_HEADS, SSM_DIM, XBC_DIM, SSM_HEADS,
                   2 * GM_DIM])
    seg = lambda i, j=None: w_in[:, :, o[i]:o[(i if j is None else j) + 1]].astype(BF16)
    pad_cols = LANES - SSM_HEADS - IDX_HEADS
    w_z = _segment_call(w_in, o[6], SSM_DIM)
    w_gm = _segment_call(w_in, o[9], 2 * GM_DIM)
    w_xbc = _segment_call(w_in, o[7], XBC_DIM)
    w_att = jnp.concatenate([seg(0, 4), seg(8), seg(5), jnp.zeros((depth, d, pad_cols), BF16)], axis=-1)
    w_gate_b = w_gate.astype(BF16)
    w_branch_b = w_branch.astype(BF16)
    w_out_b = w_out.astype(BF16)
    w_router_b = jnp.pad(w_router, ((0, 0), (0, 0), (0, LANES - N_EXPERTS))).astype(BF16)
    lane_pad = lambda a: jnp.pad(a, ((0, 0), (0, LANES - a.shape[-1])))[:, None, :]
    vec = lambda a: a[:, None, :]
    prm = dict(
        idx_kn_g=vec(idx_kn_g), idx_kn_b=vec(idx_kn_b),
        conv_w=jnp.pad(conv_w, ((0, 0), (0, 8 - CONV_W), (0, 0))), conv_b=vec(conv_b),
        dt_bias=lane_pad(dt_bias), a_log=lane_pad(a_log),
        d_skip=vec(jnp.repeat(d_skip, SSM_HEAD_DIM, axis=-1)), ssm_norm_g=vec(ssm_norm_g),
        gm_ln_g=vec(gm_ln_g), gm_ln_b=vec(gm_ln_b), gm_ws=gm_ws,
        gm_bs_t=jnp.pad(jnp.swapaxes(gm_bs, 1, 2), ((0, 0), (0, 0), (0, LANES - GM_GROUPS))),
    )
    b_router_p = lane_pad(b_router)
    b1_4 = b1[:, :, None, :]
    b2_4 = b2[:, :, None, :]
    ln = dict(g1=vec(ln1_g), b1=vec(ln1_b), g2=vec(ln2_g), b2=vec(ln2_b))

    tab_p = _rope_tables(jnp.arange(sp, dtype=I32))
    pos_s = past_len + jnp.arange(SAMPLE_ROWS, dtype=I32)
    tab_s1 = _rope_tables(pos_s)
    tab_s = tuple(jnp.tile(a, (db, 1)) for a in tab_s1)

    n_c = bp + db
    c_rows = jnp.concatenate([c_prompt, c_sample, jnp.zeros((-n_c % 8, d), F32)], axis=0)
    mod = _mod_call(c_rows, w_ada, b_ada).reshape(depth, c_rows.shape[0], 6, d)

    def mods(layer, which, grp, lo, hi):
        return grp.mod_array(mod[layer, lo:hi, which])

    zero_mod_p = grp_p.mod_array(jnp.zeros((bp, d), F32))
    zero_mod_s = grp_s.mod_array(jnp.zeros((db, d), F32))

    xs_pad = jnp.pad(x_sample, ((0, 0), (0, SAMPLE_ROWS - ds), (0, 0))).reshape(n_s, d)
    xp = x_prompt.reshape(n_p, d)
    xs_ = xs_pad
    hp = _modulate_call(grp_p, xp, mods(0, 1, grp_p, 0, bp), mods(0, 0, grp_p, 0, bp))
    hs = _modulate_call(grp_s, xs_, mods(0, 1, grp_s, bp, n_c), mods(0, 0, grp_s, bp, n_c))

    conv0 = jnp.zeros((bp, CONV_W - 1, XBC_DIM), F32)
    ssm0 = jnp.zeros((bp, SSM_HEADS * SSM_HEAD_DIM, SSM_STATE), F32)
    state_ssm4 = state_ssm.reshape(db, depth, SSM_HEADS * SSM_HEAD_DIM, SSM_STATE)

    outs_p = [[] for _ in range(5)]
    outs_s = [[] for _ in range(6)]
    for layer in range(depth):
        groups = (
            (grp_p, hp, xp, bp, sp, tab_p, sp // min(n_p, 512), None, (conv0, ssm0, False), BF16, min(n_p, 1024),
             0, bp),
            (grp_s, hs, xs_, db, SAMPLE_ROWS, tab_s, 1, (page_table, cache_k, cache_v, cache_idx_k),
             (state_conv, state_ssm4, True), F32, n_s, bp, n_c),
        )
        x1s, packs, tis, tws = [], [], [], []
        for grp, h, x, nb, t, tabs, tab_blocks, past, ssm_state, odt, tm_mm, lo, hi in groups:
            z = _mm_call(h, w_z, layer, 0, SSM_DIM, tm_mm, 512)
            gm = _mm_call(h, w_gm, layer, 0, 2 * GM_DIM, tm_mm, 512)
            xbc = _mm_call(h, w_xbc, layer, 0, XBC_DIM, tm_mm, 512)
            att = _mm_call(h, w_att, layer, 0, ATT_COLS, tm_mm, 256)
            prm_g = dict(prm, t_valid=(t if past is None else ds))
            branches, new = _mixers(grp, h, att, z, xbc, gm, nb, t, layer, prm_g, tabs, tab_blocks, past, ssm_state, odt)
            k_f, ki_f, ssm_new, conv_new, gm_v = new
            v_f = att[:, ATT_V_BLK * KV_DIM:(ATT_V_BLK + 1) * KV_DIM]
            if past is None:
                outs_p[0].append(k_f.reshape(nb, t, N_KV_HEADS, HEAD_DIM))
                outs_p[1].append(v_f.reshape(nb, t, N_KV_HEADS, HEAD_DIM))
                outs_p[2].append(ki_f.reshape(nb, t, IDX_DIM))
                outs_p[3].append(ssm_new.reshape(nb, SSM_HEADS, SSM_HEAD_DIM, SSM_STATE))
                outs_p[4].append(conv_new)
            else:
                outs_s[0].append(k_f.reshape(nb, t, N_KV_HEADS, HEAD_DIM)[:, :ds])
                outs_s[1].append(v_f.reshape(nb, t, N_KV_HEADS, HEAD_DIM)[:, :ds])
                outs_s[2].append(ki_f.reshape(nb, t, IDX_DIM)[:, :ds])
                outs_s[3].append(ssm_new.reshape(nb, SSM_HEADS, SSM_HEAD_DIM, SSM_STATE))
                outs_s[4].append(conv_new)
                outs_s[5].append(gm_v[:, :ds])
            merged = _merge_call(h, *branches, w_gate_b, w_branch_b, layer, min(grp.n, 512), 256)
            x1, pack, ti, tw = _projln_call(grp, merged, w_out_b, x, mods(layer, 2, grp, lo, hi), ln["g1"], ln["b1"],
                                            mods(layer, 4, grp, lo, hi), mods(layer, 3, grp, lo, hi),
                                            w_router_b, b_router_p, layer, alpha, 512)
            x1s.append(x1)
            packs.append(pack)
            tis.append(ti)
            tws.append(tw)
        hp_all = jnp.concatenate(packs, axis=0)
        ti_all = jnp.concatenate(tis, axis=0)
        tw_all = jnp.concatenate(tws, axis=0)
        y4 = _moe(hp_all, ti_all, w1, b1_4, w2, b2_4, layer)
        nxt = layer + 1
        new_x, new_h = [], []
        for (grp, lo, hi, zero_mod, row0, tm), x1 in zip(
                ((grp_p, 0, bp, zero_mod_p, 0, 128), (grp_s, bp, n_c, zero_mod_s, n_p, n_s)), x1s):
            sc = mods(nxt, 1, grp, lo, hi) if nxt < depth else zero_mod
            sh = mods(nxt, 0, grp, lo, hi) if nxt < depth else zero_mod
            xo, ho = _combineln_call(grp, y4, row0, tw_all, x1, mods(layer, 5, grp, lo, hi), ln["g2"], ln["b2"],
                                     sc, sh, layer, alpha, tm)
            new_x.append(xo)
            new_h.append(ho)
        (xp, xs_), (hp, hs) = new_x, new_h

    stack = lambda lst: jnp.stack(lst, axis=1)
    y_prompt = xp.reshape(bp, sp, d)
    y_sample = xs_.reshape(db, SAMPLE_ROWS, d)[:, :ds]
    return (y_prompt, y_sample, *(stack(a) for a in outs_p), *(stack(a) for a in outs_s))
```
